```python
import jax, jax.numpy as jnp
from jax import lax
import numpy as np

D_MODEL = 2048
BATCH = 4
SEQ = 4096
DEPTH = 4

D_CONV = 1024
CONV_GROUPS = 8
CONV_WIDTH = 31
N_HEADS = 8
QK_NOPE_DIM = 128
QK_ROPE_DIM = 64
V_HEAD_DIM = 128
D_ATT = N_HEADS * V_HEAD_DIM
Q_LORA_RANK = 512
KV_LORA_RANK = 256
ROPE_THETA = 10000.0
Q_BLOCK = 128
D_MIX = D_CONV + D_ATT
D_IN = 2 * D_CONV + Q_LORA_RANK + KV_LORA_RANK + QK_ROPE_DIM
D_FF = 5632
N_EXPERTS = 8
TOP_K = 2
N_DENSE = (DEPTH + 1) // 2
N_MOE = DEPTH // 2
ALPHA = (2.0 * DEPTH) ** 0.25
BETA = (8.0 * DEPTH) ** -0.25
LN_EPS = 1e-5
RMS_EPS = 1e-6

kernel_name = "hybrid_conformer_mla_moe_deepnorm"


def layer_norm(x, g, b):
    xf = x.astype(jnp.float32)
    mu = jnp.mean(xf, axis=-1, keepdims=True)
    var = jnp.mean(jnp.square(xf - mu), axis=-1, keepdims=True)
    return ((xf - mu) * lax.rsqrt(var + LN_EPS) * g.astype(jnp.float32) + b.astype(jnp.float32)).astype(x.dtype)


def rms_norm(x, g):
    xf = x.astype(jnp.float32)
    ms = jnp.mean(jnp.square(xf), axis=-1, keepdims=True)
    return (xf * lax.rsqrt(ms + RMS_EPS) * g.astype(jnp.float32)).astype(x.dtype)


def rope_tables(positions):
    inv_freq = ROPE_THETA ** (-jnp.arange(0, QK_ROPE_DIM, 2, dtype=jnp.float32) / QK_ROPE_DIM)
    ang = positions.astype(jnp.float32)[..., None] * inv_freq
    return jnp.cos(ang), jnp.sin(ang)


def apply_rope(x, cos, sin):
    x1, x2 = jnp.split(x, 2, axis=-1)
    return jnp.concatenate([x1 * cos - x2 * sin, x2 * cos + x1 * sin], axis=-1).astype(x.dtype)


def conformer_conv_group(u_conv, w_dw, gn_g, gn_b):
    b, s, _ = u_conv.shape
    a, gate = jnp.split(u_conv, 2, axis=-1)
    h = a * jax.nn.sigmoid(gate)
    hp = jnp.pad(h, ((0, 0), (CONV_WIDTH - 1, 0), (0, 0)))
    h = lax.conv_general_dilated(
        hp, w_dw[:, None, :].astype(hp.dtype), window_strides=(1,), padding='VALID',
        dimension_numbers=('NWC', 'WIO', 'NWC'), feature_group_count=D_CONV)
    hf = h.astype(jnp.float32).reshape(b, s, CONV_GROUPS, D_CONV // CONV_GROUPS)
    mu = jnp.mean(hf, axis=-1, keepdims=True)
    var = jnp.mean(jnp.square(hf - mu), axis=-1, keepdims=True)
    hf = ((hf - mu) * lax.rsqrt(var + LN_EPS)).reshape(b, s, D_CONV)
    h = (hf * gn_g.astype(jnp.float32) + gn_b.astype(jnp.float32)).astype(u_conv.dtype)
    return jax.nn.silu(h)


def causal_mla_attention(q_nope, q_rope, k_nope, k_rope, v):
    b, s = q_nope.shape[0], q_nope.shape[1]
    n_blk = s // Q_BLOCK
    scale = (QK_NOPE_DIM + QK_ROPE_DIM) ** -0.5
    qn = q_nope.reshape(b, n_blk, Q_BLOCK, N_HEADS, QK_NOPE_DIM).transpose(1, 0, 2, 3, 4)
    qr = q_rope.reshape(b, n_blk, Q_BLOCK, N_HEADS, QK_ROPE_DIM).transpose(1, 0, 2, 3, 4)
    kpos = jnp.arange(s)

    def one_block(args):
        i, qn_b, qr_b = args
        sc = (jnp.einsum('bqhd,bkhd->bhqk', qn_b, k_nope, preferred_element_type=jnp.float32)
              + jnp.einsum('bqhd,bkd->bhqk', qr_b, k_rope, preferred_element_type=jnp.float32)) * scale
        qpos = i * Q_BLOCK + jnp.arange(Q_BLOCK)
        mask = kpos[None, :] <= qpos[:, None]
        sc = jnp.where(mask[None, None], sc, -jnp.inf)
        p = jax.nn.softmax(sc, axis=-1)
        return jnp.einsum('bhqk,bkhd->bqhd', p.astype(v.dtype), v)

    out = lax.map(one_block, (jnp.arange(n_blk), qn, qr))
    return out.transpose(1, 0, 2, 3, 4).reshape(b, s, D_ATT)


def mla_group(u_q, u_kv, u_kr, cos, sin, q_norm_g, w_q_b, kv_norm_g, w_kv_b):
    b, s, _ = u_q.shape
    q_lat = rms_norm(u_q, q_norm_g)
    q = jnp.einsum('bsr,rf->bsf', q_lat, w_q_b).reshape(b, s, N_HEADS, QK_NOPE_DIM + QK_ROPE_DIM)
    q_nope, q_rope = jnp.split(q, [QK_NOPE_DIM], axis=-1)
    q_rope = apply_rope(q_rope, cos[:, :, None, :], sin[:, :, None, :])
    kv_lat = rms_norm(u_kv, kv_norm_g)
    kv = jnp.einsum('bsr,rf->bsf', kv_lat, w_kv_b).reshape(b, s, N_HEADS, QK_NOPE_DIM + V_HEAD_DIM)
    k_nope, v = jnp.split(kv, [QK_NOPE_DIM], axis=-1)
    k_rope = apply_rope(u_kr, cos, sin)
    return causal_mla_attention(q_nope, q_rope, k_nope, k_rope, v)


def hybrid_mixer(x, cos, sin, w_in, w_dw, gn_g, gn_b, q_norm_g, w_q_b, kv_norm_g, w_kv_b, w_o):
    u = jnp.einsum('bsd,df->bsf', x, w_in)
    u_conv, u_q, u_kv, u_kr = jnp.split(
        u, [2 * D_CONV, 2 * D_CONV + Q_LORA_RANK, 2 * D_CONV + Q_LORA_RANK + KV_LORA_RANK], axis=-1)
    y_conv = conformer_conv_group(u_conv, w_dw, gn_g, gn_b)
    y_att = mla_group(u_q, u_kv, u_kr, cos, sin, q_norm_g, w_q_b, kv_norm_g, w_kv_b)
    y = jnp.concatenate([y_conv, y_att], axis=-1)
    return jnp.einsum('bsf,fd->bsd', y, w_o)


def swiglu(x, w_gate, w_up, w_down):
    return (jax.nn.silu(x @ w_gate) * (x @ w_up)) @ w_down


def moe_swiglu(x, w_router, w_gate, w_up, w_down):
    xt = x.reshape(-1, D_MODEL)
    logits = (xt @ w_router).astype(jnp.float32)
    top_val, top_idx = lax.top_k(logits, TOP_K)
    gates = jax.nn.softmax(top_val, axis=-1)
    combine = jnp.sum(jax.nn.one_hot(top_idx, N_EXPERTS, dtype=jnp.float32) * gates[..., None], axis=1)
    y = jnp.zeros_like(xt)
    for e in range(N_EXPERTS):
        y = y + combine[:, e:e + 1].astype(xt.dtype) * swiglu(xt, w_gate[e], w_up[e], w_down[e])
    return y.reshape(x.shape)


def setup_inputs(seed: int = 0) -> dict:
    key = jax.random.key(seed)
    ks = jax.random.split(key, 24)
    f32 = jnp.float32

    def nrm(k, shape, std):
        return jax.random.normal(k, shape, f32) * std

    x = jax.random.normal(ks[0], (BATCH, SEQ, D_MODEL), f32)
    offset = jax.random.randint(ks[1], (BATCH, 1), 0, 1024, dtype=jnp.int32)
    positions = offset + jnp.arange(SEQ, dtype=jnp.int32)[None, :]
    return {
        "x": x,
        "positions": positions,
        "w_in": nrm(ks[2], (DEPTH, D_MODEL, D_IN), D_MODEL ** -0.5),
        "conv_dw": nrm(ks[3], (DEPTH, CONV_WIDTH, D_CONV), CONV_WIDTH ** -0.5),
        "conv_gn_g": 1.0 + nrm(ks[4], (DEPTH, D_CONV), 0.01),
        "conv_gn_b": nrm(ks[5], (DEPTH, D_CONV), 0.01),
        "q_norm_g": 1.0 + nrm(ks[6], (DEPTH, Q_LORA_RANK), 0.01),
        "w_q_b": nrm(ks[7], (DEPTH, Q_LORA_RANK, N_HEADS * (QK_NOPE_DIM + QK_ROPE_DIM)), Q_LORA_RANK ** -0.5),
        "kv_norm_g": 1.0 + nrm(ks[8], (DEPTH, KV_LORA_RANK), 0.01),
        "w_kv_b": nrm(ks[9], (DEPTH, KV_LORA_RANK, N_HEADS * (QK_NOPE_DIM + V_HEAD_DIM)), KV_LORA_RANK ** -0.5),
        "w_o": nrm(ks[10], (DEPTH, D_MIX, D_MODEL), BETA * D_MIX ** -0.5),
        "ln_mix_g": 1.0 + nrm(ks[11], (DEPTH, D_MODEL), 0.01),
        "ln_mix_b": nrm(ks[12], (DEPTH, D_MODEL), 0.01),
        "ffn_w_gate": nrm(ks[13], (N_DENSE, D_MODEL, D_FF), D_MODEL ** -0.5),
        "ffn_w_up": nrm(ks[14], (N_DENSE, D_MODEL, D_FF), D_MODEL ** -0.5),
        "ffn_w_down": nrm(ks[15], (N_DENSE, D_FF, D_MODEL), BETA * D_FF ** -0.5),
        "router_w": nrm(ks[16], (N_MOE, D_MODEL, N_EXPERTS), D_MODEL ** -0.5),
        "exp_w_gate": nrm(ks[17], (N_MOE, N_EXPERTS, D_MODEL, D_FF), D_MODEL ** -0.5),
        "exp_w_up": nrm(ks[18], (N_MOE, N_EXPERTS, D_MODEL, D_FF), D_MODEL ** -0.5),
        "exp_w_down": nrm(ks[19], (N_MOE, N_EXPERTS, D_FF, D_MODEL), BETA * D_FF ** -0.5),
        "ln_ffn_g": 1.0 + nrm(ks[20], (DEPTH, D_MODEL), 0.01),
        "ln_ffn_b": nrm(ks[21], (DEPTH, D_MODEL), 0.01),
    }


def reference(x, positions, w_in, conv_dw, conv_gn_g, conv_gn_b, q_norm_g, w_q_b, kv_norm_g, w_kv_b,
              w_o, ln_mix_g, ln_mix_b, ffn_w_gate, ffn_w_up, ffn_w_down, router_w, exp_w_gate,
              exp_w_up, exp_w_down, ln_ffn_g, ln_ffn_b):
    cos, sin = rope_tables(positions)
    for l in range(DEPTH):
        y = hybrid_mixer(x, cos, sin, w_in[l], conv_dw[l], conv_gn_g[l], conv_gn_b[l],
                         q_norm_g[l], w_q_b[l], kv_norm_g[l], w_kv_b[l], w_o[l])
        x = layer_norm(ALPHA * x + y, ln_mix_g[l], ln_mix_b[l])
        j = l // 2
        if l % 2 == 0:
            f = swiglu(x, ffn_w_gate[j], ffn_w_up[j], ffn_w_down[j])
        else:
            f = moe_swiglu(x, router_w[j], exp_w_gate[j], exp_w_up[j], exp_w_down[j])
        x = layer_norm(ALPHA * x + f, ln_ffn_g[l], ln_ffn_b[l])
    return x
```

```python
import functools

import jax
import jax.numpy as jnp
from jax import lax
from jax.experimental import pallas as pl
from jax.experimental.pallas import tpu as pltpu

F32 = jnp.float32
BF16 = jnp.bfloat16

CONV_GROUPS = 8
CONV_WIDTH = 31
N_HEADS = 8
QK_NOPE_DIM = 128
QK_ROPE_DIM = 64
V_HEAD_DIM = 128
Q_LORA_RANK = 512
KV_LORA_RANK = 256
ROPE_THETA = 10000.0
N_EXPERTS = 8
DEPTH = 4
ALPHA = (2.0 * DEPTH) ** 0.25
LN_EPS = 1e-5
RMS_EPS = 1e-6

LANES = 128
SUBLANES = 8
VMEM_LIMIT = 56 * 1024 * 1024

HEAD_PAD = 2 * LANES
CONV_HALO = 32

TM_PROJ = 256
TS_CONV = 128
TQ_ATT = 512
TM_OUT = 512
TM_FFN = 512
TF_FFN = 512
TM_ROUTE = 512
TM_EXP = 512
TM_COMB = 256
SLAB = 2048 // LANES


def _cparams(sem):
    return pltpu.CompilerParams(dimension_semantics=sem, vmem_limit_bytes=VMEM_LIMIT)


def _const_spec(shape):
    nd = len(shape)
    return pl.BlockSpec(shape, lambda *_: (0,) * nd, pipeline_mode=pl.Buffered(1))


def _layer_norm(z, g, b):
    mu = jnp.mean(z, axis=-1, keepdims=True)
    d = z - mu
    var = jnp.mean(d * d, axis=-1, keepdims=True)
    return d * lax.rsqrt(var + LN_EPS) * g + b


def _rope(x, c, sn, sp):
    return x * c + pltpu.roll(x, LANES - QK_ROPE_DIM // 2, 1) * sn + pltpu.roll(x, QK_ROPE_DIM // 2, 1) * sp


def _rope_table_body(pos_ref, invf_ref, c_ref, sn_ref, sp_ref):
    ang = pos_ref[...].astype(F32) * invf_ref[...]
    c = jnp.cos(ang)
    s = jnp.sin(ang)
    lane = lax.broadcasted_iota(jnp.int32, ang.shape, 1)
    half = QK_ROPE_DIM // 2
    c_ref[...] = jnp.where(lane < QK_ROPE_DIM, c, 0.0)
    sn_ref[...] = jnp.where(lane < half, -s, 0.0)
    sp_ref[...] = jnp.where((lane >= half) & (lane < QK_ROPE_DIM), s, 0.0)


def rope_tables(pos_col):
    t = pos_col.shape[0]
    tm = 1024
    half = QK_ROPE_DIM // 2
    inv_freq = ROPE_THETA ** (-jnp.arange(0, QK_ROPE_DIM, 2, dtype=F32) / QK_ROPE_DIM)
    invf = jnp.concatenate([inv_freq, inv_freq, jnp.zeros((LANES - 2 * half,), F32)]).reshape(1, LANES)
    out = jax.ShapeDtypeStruct((t, LANES), F32)
    row = pl.BlockSpec((tm, LANES), lambda i: (i, 0))
    return pl.pallas_call(
        _rope_table_body,
        grid=(t // tm,),
        in_specs=[pl.BlockSpec((tm, 1), lambda i: (i, 0)), pl.BlockSpec((1, LANES), lambda i: (0, 0))],
        out_specs=[row, row, row],
        out_shape=[out, out, out],
        compiler_params=_cparams(("arbitrary",)),
        name="rope_tables",
    )(pos_col, invf)


def _in_proj_body(x_ref, win_ref, qg_ref, wq_ref, kvg_ref, wk_ref, wv_ref, c_ref, sn_ref, sp_ref,
                  uconv_ref, q_ref, k_ref, v_ref, *, d_conv2, scale):
    xb = x_ref[...].astype(BF16)
    c, sn, sp = c_ref[...], sn_ref[...], sp_ref[...]
    o_q = d_conv2
    o_kv = o_q + Q_LORA_RANK
    o_kr = o_kv + KV_LORA_RANK

    uconv_ref[...] = jnp.dot(xb, win_ref[:, :d_conv2], preferred_element_type=F32)

    uq = jnp.dot(xb, win_ref[:, o_q:o_kv], preferred_element_type=F32)
    ms = jnp.mean(uq * uq, axis=-1, keepdims=True)
    q_lat = (uq * lax.rsqrt(ms + RMS_EPS) * qg_ref[...]).astype(BF16)
    q = jnp.dot(q_lat, wq_ref[...], preferred_element_type=F32) * scale
    for h in range(N_HEADS):
        lo = h * HEAD_PAD
        q_ref[:, lo:lo + LANES] = q[:, lo:lo + LANES].astype(BF16)
        q_ref[:, lo + LANES:lo + HEAD_PAD] = _rope(q[:, lo + LANES:lo + HEAD_PAD], c, sn, sp).astype(BF16)

    ukv = jnp.dot(xb, win_ref[:, o_kv:o_kr], preferred_element_type=F32)
    ms = jnp.mean(ukv * ukv, axis=-1, keepdims=True)
    kv_lat = (ukv * lax.rsqrt(ms + RMS_EPS) * kvg_ref[...]).astype(BF16)
    v_ref[...] = jnp.dot(kv_lat, wv_ref[...], preferred_element_type=F32).astype(BF16)
    kn = jnp.dot(kv_lat, wk_ref[...], preferred_element_type=F32)
    ukr = jnp.dot(xb, win_ref[:, o_kr:o_kr + LANES], preferred_element_type=F32)
    kr = _rope(ukr, c, sn, sp).astype(BF16)
    for h in range(N_HEADS):
        lo = h * HEAD_PAD
        k_ref[:, lo:lo + LANES] = kn[:, h * LANES:(h + 1) * LANES].astype(BF16)
        k_ref[:, lo + LANES:lo + HEAD_PAD] = kr


def in_proj(x2d, win, qg, wq, kvg, wk, wv, rope_c, rope_sn, rope_sp):
    t, d = x2d.shape
    tm = TM_PROJ
    d_conv2 = win.shape[1] - (Q_LORA_RANK + KV_LORA_RANK + LANES)
    scale = float((QK_NOPE_DIM + QK_ROPE_DIM) ** -0.5)
    row = lambda w: pl.BlockSpec((tm, w), lambda i: (i, 0))
    return pl.pallas_call(
        functools.partial(_in_proj_body, d_conv2=d_conv2, scale=scale),
        grid=(t // tm,),
        in_specs=[row(d), _const_spec(win.shape), _const_spec(qg.shape), _const_spec(wq.shape),
                  _const_spec(kvg.shape), _const_spec(wk.shape), _const_spec(wv.shape),
                  row(LANES), row(LANES), row(LANES)],
        out_specs=[row(d_conv2), row(N_HEADS * HEAD_PAD), row(N_HEADS * HEAD_PAD), row(N_HEADS * V_HEAD_DIM)],
        out_shape=[jax.ShapeDtypeStruct((t, d_conv2), F32),
                   jax.ShapeDtypeStruct((t, N_HEADS * HEAD_PAD), BF16),
                   jax.ShapeDtypeStruct((t, N_HEADS * HEAD_PAD), BF16),
                   jax.ShapeDtypeStruct((t, N_HEADS * V_HEAD_DIM), BF16)],
        compiler_params=_cparams(("arbitrary",)),
        name="in_proj",
    )(x2d, win, qg, wq, kvg, wk, wv, rope_c, rope_sn, rope_sp)


def _conv_body(a_ref, g_ref, w_ref, gng_ref, gnb_ref, o_ref, hext):
    ts = a_ref.shape[1]
    s = pl.program_id(1)

    @pl.when(s == 0)
    def _():
        hext[0:CONV_HALO, :] = jnp.zeros((CONV_HALO, hext.shape[1]), F32)

    @pl.when(s > 0)
    def _():
        hext[0:CONV_HALO, :] = hext[ts:ts + CONV_HALO, :]

    hext[CONV_HALO:CONV_HALO + ts, :] = a_ref[0] * jax.nn.sigmoid(g_ref[0])

    first = CONV_HALO - (CONV_WIDTH - 1)
    n_groups = hext.shape[1] // LANES
    for c in range(n_groups):
        lanes = slice(c * LANES, (c + 1) * LANES)
        acc = hext[first:first + ts, lanes] * w_ref[0:1, lanes]
        for k in range(1, CONV_WIDTH):
            acc = acc + hext[first + k:first + k + ts, lanes] * w_ref[k:k + 1, lanes]
        mu = jnp.mean(acc, axis=-1, keepdims=True)
        d = acc - mu
        var = jnp.mean(d * d, axis=-1, keepdims=True)
        y = d * lax.rsqrt(var + LN_EPS) * gng_ref[:, lanes] + gnb_ref[:, lanes]
        o_ref[0, :, lanes] = (y * jax.nn.sigmoid(y)).astype(BF16)


def conv_module(u_conv, w_dw, gn_g, gn_b):
    b, s, d2 = u_conv.shape
    dc = d2 // 2
    assert dc // CONV_GROUPS == LANES
    ts = TS_CONV
    return pl.pallas_call(
        _conv_body,
        grid=(b, s // ts),
        in_specs=[pl.BlockSpec((1, ts, dc), lambda i, j: (i, j, 0)),
                  pl.BlockSpec((1, ts, dc), lambda i, j: (i, j, 1)),
                  pl.BlockSpec((CONV_WIDTH, dc), lambda i, j: (0, 0)),
                  pl.BlockSpec((1, dc), lambda i, j: (0, 0)),
                  pl.BlockSpec((1, dc), lambda i, j: (0, 0))],
        out_specs=pl.BlockSpec((1, ts, dc), lambda i, j: (i, j, 0)),
        out_shape=jax.ShapeDtypeStruct((b, s, dc), BF16),
        scratch_shapes=[pltpu.VMEM((CONV_HALO + ts, dc), F32)],
        compiler_params=_cparams(("arbitrary", "arbitrary")),
        name="conv_module",
    )(u_conv, u_conv, w_dw, gn_g, gn_b)


_NT = (((1,), (1,)), ((), ()))


def _attn_body(q_ref, k_ref, v_ref, o_ref, m_sc, l_sc, acc_sc):
    tq = q_ref.shape[1]
    i = pl.program_id(2)
    q = q_ref[0]
    m_sc[...] = jnp.full(m_sc.shape, -jnp.inf, F32)
    l_sc[...] = jnp.zeros(l_sc.shape, F32)
    acc_sc[...] = jnp.zeros(acc_sc.shape, F32)

    def block(j, masked):
        start = pl.multiple_of(j * tq, tq)
        kj = k_ref[0, pl.ds(start, tq), :]
        vj = v_ref[0, pl.ds(start, tq), :]
        s = lax.dot_general(q, kj, _NT, preferred_element_type=F32)
        if masked:
            row = lax.broadcasted_iota(jnp.int32, s.shape, 0)
            col = lax.broadcasted_iota(jnp.int32, s.shape, 1)
            s = jnp.where(col <= row, s, -jnp.inf)
        m_prev = m_sc[...]
        m_new = jnp.maximum(m_prev, jnp.max(s, axis=-1, keepdims=True))
        a = jnp.exp(m_prev - m_new)
        p = jnp.exp(s - m_new)
        l_sc[...] = a * l_sc[...] + jnp.sum(p, axis=-1, keepdims=True)
        acc_sc[...] = a * acc_sc[...] + jnp.dot(p.astype(BF16), vj, preferred_element_type=F32)
        m_sc[...] = m_new

    def body(j, carry):
        block(j, False)
        return carry

    lax.fori_loop(0, i, body, 0)
    block(i, True)
    o_ref[0] = (acc_sc[...] / l_sc[...]).astype(BF16)


def attention(q, k, v):
    b, s, _ = q.shape
    tq = TQ_ATT
    return pl.pallas_call(
        _attn_body,
        grid=(b, N_HEADS, s // tq),
        in_specs=[pl.BlockSpec((1, tq, HEAD_PAD), lambda bi, h, i: (bi, i, h)),
                  pl.BlockSpec((1, s, HEAD_PAD), lambda bi, h, i: (bi, 0, h)),
                  pl.BlockSpec((1, s, V_HEAD_DIM), lambda bi, h, i: (bi, 0, h))],
        out_specs=pl.BlockSpec((1, tq, V_HEAD_DIM), lambda bi, h, i: (bi, i, h)),
        out_shape=jax.ShapeDtypeStruct((b, s, N_HEADS * V_HEAD_DIM), BF16),
        scratch_shapes=[pltpu.VMEM((tq, 1), F32), pltpu.VMEM((tq, 1), F32), pltpu.VMEM((tq, V_HEAD_DIM), F32)],
        compiler_params=_cparams(("arbitrary", "arbitrary", "arbitrary")),
        name="attention",
    )(q, k, v)


def _out_proj_body(yc_ref, ya_ref, wc_ref, wa_ref, x_ref, g_ref, b_ref, o_ref):
    y = jnp.dot(yc_ref[...], wc_ref[...], preferred_element_type=F32)
    y = y + jnp.dot(ya_ref[...], wa_ref[...], preferred_element_type=F32)
    o_ref[...] = _layer_norm(ALPHA * x_ref[...] + y, g_ref[...], b_ref[...])


def out_proj(y_conv, y_att, wo_conv, wo_att, x2d, g, b):
    t, d = x2d.shape
    tm = TM_OUT
    row = lambda w: pl.BlockSpec((tm, w), lambda i: (i, 0))
    return pl.pallas_call(
        _out_proj_body,
        grid=(t // tm,),
        in_specs=[row(y_conv.shape[1]), row(y_att.shape[1]), _const_spec(wo_conv.shape), _const_spec(wo_att.shape),
                  row(d), _const_spec(g.shape), _const_spec(b.shape)],
        out_specs=row(d),
        out_shape=jax.ShapeDtypeStruct((t, d), F32),
        compiler_params=_cparams(("arbitrary",)),
        name="out_proj",
    )(y_conv, y_att, wo_conv, wo_att, x2d, g, b)


def _swiglu_chunk(xb, wg, wu, wd):
    g = jnp.dot(xb, wg, preferred_element_type=F32)
    u = jnp.dot(xb, wu, preferred_element_type=F32)
    h = (g * jax.nn.sigmoid(g) * u).astype(BF16)
    return jnp.dot(h, wd, preferred_element_type=F32)


def _ffn_body(x_ref, wg_ref, wu_ref, wd_ref, g_ref, b_ref, o_ref, xb_sc, acc_sc):
    f = pl.program_id(1)

    @pl.when(f == 0)
    def _():
        xb_sc[...] = x_ref[...].astype(BF16)
        acc_sc[...] = jnp.zeros(acc_sc.shape, F32)

    acc_sc[...] += _swiglu_chunk(xb_sc[...], wg_ref[...], wu_ref[...], wd_ref[...])

    @pl.when(f == pl.num_programs(1) - 1)
    def _():
        o_ref[...] = _layer_norm(ALPHA * x_ref[...] + acc_sc[...], g_ref[...], b_ref[...])


def dense_ffn(x2d, wg, wu, wd, g, b):
    t, d = x2d.shape
    dff = wg.shape[1]
    tm, tf = TM_FFN, TF_FFN
    return pl.pallas_call(
        _ffn_body,
        grid=(t // tm, dff // tf),
        in_specs=[pl.BlockSpec((tm, d), lambda i, f: (i, 0)),
                  pl.BlockSpec((d, tf), lambda i, f: (0, f)),
                  pl.BlockSpec((d, tf), lambda i, f: (0, f)),
                  pl.BlockSpec((tf, d), lambda i, f: (f, 0)),
                  pl.BlockSpec((1, d), lambda i, f: (0, 0)),
                  pl.BlockSpec((1, d), lambda i, f: (0, 0))],
        out_specs=pl.BlockSpec((tm, d), lambda i, f: (i, 0)),
        out_shape=jax.ShapeDtypeStruct((t, d), F32),
        scratch_shapes=[pltpu.VMEM((tm, d), BF16), pltpu.VMEM((tm, d), F32)],
        compiler_params=_cparams(("arbitrary", "arbitrary")),
        name="dense_ffn",
    )(x2d, wg, wu, wd, g, b)


_L_IDX, _L_RANK, _L_GATE = 0, 2, 4


def _router_body(x_ref, wr_ref, o_ref, cnt_ref, base_sc):
    tm = x_ref.shape[0]

    @pl.when(pl.program_id(0) == 0)
    def _():
        base_sc[...] = jnp.zeros(base_sc.shape, F32)

    logits = jnp.dot(x_ref[...], wr_ref[...], preferred_element_type=F32, precision=lax.Precision.HIGHEST)
    lane = lax.broadcasted_iota(jnp.int32, logits.shape, 1).astype(F32)
    lg = jnp.where(lane < N_EXPERTS, logits, -jnp.inf)
    m1 = jnp.max(lg, axis=-1, keepdims=True)
    i1 = jnp.min(jnp.where(lg == m1, lane, float(LANES)), axis=-1, keepdims=True)
    oh1 = lane == i1
    lg2 = jnp.where(oh1, -jnp.inf, lg)
    m2 = jnp.max(lg2, axis=-1, keepdims=True)
    i2 = jnp.min(jnp.where(lg2 == m2, lane, float(LANES)), axis=-1, keepdims=True)
    oh2 = lane == i2
    e2 = jnp.exp(m2 - m1)
    g1 = 1.0 / (1.0 + e2)
    g2 = e2 / (1.0 + e2)
    cnt = jnp.where(oh1 | oh2, 1.0, 0.0)
    r = lax.broadcasted_iota(jnp.int32, (tm, tm), 0)
    c = lax.broadcasted_iota(jnp.int32, (tm, tm), 1)
    strict_lower = jnp.where(c < r, 1.0, 0.0).astype(BF16)
    before = jnp.dot(strict_lower, cnt.astype(BF16), preferred_element_type=F32) + base_sc[...]
    r1 = jnp.sum(jnp.where(oh1, before, 0.0), axis=-1, keepdims=True)
    r2 = jnp.sum(jnp.where(oh2, before, 0.0), axis=-1, keepdims=True)
    base_sc[...] += jnp.sum(cnt, axis=0, keepdims=True)
    cnt_ref[...] = base_sc[...]
    packed = jnp.zeros(logits.shape, F32)
    for ln, val in ((_L_IDX, i1), (_L_IDX + 1, i2), (_L_RANK, r1), (_L_RANK + 1, r2), (_L_GATE, g1), (_L_GATE + 1, g2)):
        packed = jnp.where(lane == float(ln), val, packed)
    o_ref[...] = packed


def router(x2d, w_router_pad):
    t, d = x2d.shape
    tm = TM_ROUTE
    return pl.pallas_call(
        _router_body,
        grid=(t // tm,),
        in_specs=[pl.BlockSpec((tm, d), lambda i: (i, 0)), pl.BlockSpec((d, LANES), lambda i: (0, 0))],
        out_specs=[pl.BlockSpec((tm, LANES), lambda i: (i, 0)), pl.BlockSpec((1, LANES), lambda i: (0, 0))],
        out_shape=[jax.ShapeDtypeStruct((t, LANES), F32), jax.ShapeDtypeStruct((1, LANES), F32)],
        scratch_shapes=[pltpu.VMEM((1, LANES), F32)],
        compiler_params=_cparams(("arbitrary",)),
        name="router",
    )(x2d, w_router_pad)


def _invert_body(pos_ref, o_ref):
    def zero(r, carry):
        o_ref[r] = 0
        return carry

    lax.fori_loop(0, o_ref.shape[0], zero, 0)

    def put(a, carry):
        o_ref[pos_ref[a]] = a // 2
        return carry

    lax.fori_loop(0, pos_ref.shape[0], put, 0)


def invert_positions(pos_flat, n_rows):
    return pl.pallas_call(
        _invert_body,
        in_specs=[pl.BlockSpec(memory_space=pltpu.SMEM)],
        out_specs=pl.BlockSpec(memory_space=pltpu.SMEM),
        out_shape=jax.ShapeDtypeStruct((n_rows,), jnp.int32),
        name="invert_positions",
    )(pos_flat)


def _slab_rows(row):
    start = row * SLAB
    return pl.ds(start if isinstance(start, int) else pl.multiple_of(start, SLAB), SLAB)


def _slab_copy(src_hbm, src_row, dst, dst_row, sem):
    return pltpu.make_async_copy(src_hbm.at[_slab_rows(src_row), :], dst.at[_slab_rows(dst_row), :], sem)


def _experts_body(te_ref, nv_ref, rt_ref, x_hbm, wg_ref, wu_ref, wd_ref, o_ref, gbuf, xb_sc, acc_sc, sem):
    tm = xb_sc.shape[0]
    i = pl.program_id(0)
    f = pl.program_id(1)
    valid = i < nv_ref[0]

    @pl.when(valid & (f == 0))
    def _():
        def issue(r, carry):
            _slab_copy(x_hbm, rt_ref[i * tm + r], gbuf, r, sem).start()
            return carry

        lax.fori_loop(0, tm, issue, 0)

        def wait(r, carry):
            _slab_copy(x_hbm, 0, gbuf, r, sem).wait()
            return carry

        lax.fori_loop(0, tm, wait, 0)
        for c in range(SLAB):
            xb_sc[:, c * LANES:(c + 1) * LANES] = gbuf[pl.ds(c, tm, stride=SLAB), :].astype(BF16)
        acc_sc[...] = jnp.zeros(acc_sc.shape, F32)

    @pl.when(jnp.logical_not(valid) & (f == 0))
    def _():
        o_ref[...] = jnp.zeros(o_ref.shape, F32)

    @pl.when(valid)
    def _():
        acc_sc[...] += _swiglu_chunk(xb_sc[...], wg_ref[...], wu_ref[...], wd_ref[...])

    @pl.when(valid & (f == pl.num_programs(1) - 1))
    def _():
        for c in range(SLAB):
            o_ref[pl.ds(c, tm, stride=SLAB), :] = acc_sc[:, c * LANES:(c + 1) * LANES]


def grouped_experts(tile_expert, n_valid, row_token, x_slab, wg, wu, wd):
    n_tiles = tile_expert.shape[0]
    _, d, dff = wg.shape
    tm, tf = TM_EXP, TF_FFN
    nf = dff // tf

    def f_idx(i, f, nv):
        return jnp.where(i < nv[0], f, nf - 1)

    return pl.pallas_call(
        _experts_body,
        grid_spec=pltpu.PrefetchScalarGridSpec(
            num_scalar_prefetch=3,
            grid=(n_tiles, nf),
            in_specs=[pl.BlockSpec(memory_space=pl.ANY),
                      pl.BlockSpec((None, d, tf), lambda i, f, te, nv, rt: (te[i], 0, f_idx(i, f, nv))),
                      pl.BlockSpec((None, d, tf), lambda i, f, te, nv, rt: (te[i], 0, f_idx(i, f, nv))),
                      pl.BlockSpec((None, tf, d), lambda i, f, te, nv, rt: (te[i], f_idx(i, f, nv), 0))],
            out_specs=pl.BlockSpec((tm * SLAB, LANES), lambda i, f, te, nv, rt: (i, 0)),
            scratch_shapes=[pltpu.VMEM((tm * SLAB, LANES), F32), pltpu.VMEM((tm, d), BF16),
                            pltpu.VMEM((tm, d), F32), pltpu.SemaphoreType.DMA],
        ),
        out_shape=jax.ShapeDtypeStruct((n_tiles * tm * SLAB, LANES), F32),
        compiler_params=_cparams(("arbitrary", "arbitrary")),
        name="grouped_experts",
    )(tile_expert, n_valid, row_token, x_slab, wg, wu, wd)


def _combine_body(pos_ref, y_hbm, x_ref, rt_ref, g_ref, b_ref, o_ref, gbuf, z_sc, sem):
    tm = x_ref.shape[0]
    i = pl.program_id(0)

    def issue(r, carry):
        t = i * tm + r
        _slab_copy(y_hbm, pos_ref[2 * t], gbuf, r, sem).start()
        _slab_copy(y_hbm, pos_ref[2 * t + 1], gbuf, tm + r, sem).start()
        return carry

    lax.fori_loop(0, tm, issue, 0)

    def wait(r, carry):
        _slab_copy(y_hbm, 0, gbuf, r, sem).wait()
        return carry

    lax.fori_loop(0, 2 * tm, wait, 0)
    g1 = rt_ref[:, _L_GATE:_L_GATE + 1]
    g2 = rt_ref[:, _L_GATE + 1:_L_GATE + 2]
    for c in range(SLAB):
        lanes = slice(c * LANES, (c + 1) * LANES)
        y1 = gbuf[pl.ds(c, tm, stride=SLAB), :]
        y2 = gbuf[pl.ds(tm * SLAB + c, tm, stride=SLAB), :]
        z_sc[:, lanes] = ALPHA * x_ref[:, lanes] + (g1 * y1 + g2 * y2)
    o_ref[...] = _layer_norm(z_sc[...], g_ref[...], b_ref[...])


def combine(pos_flat, y_slab, x2d, routed, g, b):
    t, d = x2d.shape
    tm = TM_COMB
    return pl.pallas_call(
        _combine_body,
        grid_spec=pltpu.PrefetchScalarGridSpec(
            num_scalar_prefetch=1,
            grid=(t // tm,),
            in_specs=[pl.BlockSpec(memory_space=pl.ANY),
                      pl.BlockSpec((tm, d), lambda i, pos: (i, 0)),
                      pl.BlockSpec((tm, LANES), lambda i, pos: (i, 0)),
                      pl.BlockSpec((1, d), lambda i, pos: (0, 0)),
                      pl.BlockSpec((1, d), lambda i, pos: (0, 0))],
            out_specs=pl.BlockSpec((tm, d), lambda i, pos: (i, 0)),
            scratch_shapes=[pltpu.VMEM((2 * tm * SLAB, LANES), F32), pltpu.VMEM((tm, d), F32),
                            pltpu.SemaphoreType.DMA],
        ),
        out_shape=jax.ShapeDtypeStruct((t, d), F32),
        compiler_params=_cparams(("arbitrary",)),
        name="moe_combine",
    )(pos_flat, y_slab, x2d, routed, g, b)


def moe_ffn(x2d, w_router_pad, wg, wu, wd, g, b):
    t, d = x2d.shape
    tm = TM_EXP
    n_tiles = (t * 2) // tm + N_EXPERTS
    routed, counts = router(x2d, w_router_pad)
    idx = routed[:, _L_IDX:_L_IDX + 2].astype(jnp.int32)
    rank = routed[:, _L_RANK:_L_RANK + 2].astype(jnp.int32)
    counts = counts[0, :N_EXPERTS].astype(jnp.int32)
    tiles_per_e = (counts + tm - 1) // tm
    tile_end = jnp.cumsum(tiles_per_e)
    row_start = (tile_end - tiles_per_e) * tm
    pos = (row_start[idx] + rank).reshape(-1)
    n_valid = tile_end[-1:]
    tile_id = jnp.minimum(jnp.arange(n_tiles, dtype=jnp.int32), n_valid[0] - 1)
    tile_expert = jnp.sum(tile_id[:, None] >= tile_end[None, :], axis=1).astype(jnp.int32)
    row_token = invert_positions(pos, n_tiles * tm)
    y_slab = grouped_experts(tile_expert, n_valid.astype(jnp.int32), row_token,
                             x2d.reshape(t * SLAB, LANES), wg, wu, wd)
    return combine(pos, y_slab, x2d, routed, g, b)


def kernel(x, positions, w_in, conv_dw, conv_gn_g, conv_gn_b, q_norm_g, w_q_b, kv_norm_g, w_kv_b, w_o, ln_mix_g, ln_mix_b, ffn_w_gate, ffn_w_up, ffn_w_down, router_w, exp_w_gate, exp_w_up, exp_w_down, ln_ffn_g, ln_ffn_b):
    bsz, seq, d = x.shape
    assert d == SLAB * LANES
    t = bsz * seq
    depth = w_in.shape[0]
    d_conv = conv_dw.shape[2]

    win = jnp.pad(w_in, ((0, 0), (0, 0), (0, LANES - QK_ROPE_DIM))).astype(BF16)
    wq = w_q_b.reshape(depth, Q_LORA_RANK, N_HEADS, QK_NOPE_DIM + QK_ROPE_DIM)
    wq = jnp.pad(wq, ((0, 0), (0, 0), (0, 0), (0, HEAD_PAD - QK_NOPE_DIM - QK_ROPE_DIM)))
    wq = wq.reshape(depth, Q_LORA_RANK, N_HEADS * HEAD_PAD).astype(BF16)
    wkv = w_kv_b.reshape(depth, KV_LORA_RANK, N_HEADS, QK_NOPE_DIM + V_HEAD_DIM)
    wk = wkv[..., :QK_NOPE_DIM].reshape(depth, KV_LORA_RANK, N_HEADS * QK_NOPE_DIM).astype(BF16)
    wv = wkv[..., QK_NOPE_DIM:].reshape(depth, KV_LORA_RANK, N_HEADS * V_HEAD_DIM).astype(BF16)
    wo = w_o.astype(BF16)
    fwg, fwu, fwd = ffn_w_gate.astype(BF16), ffn_w_up.astype(BF16), ffn_w_down.astype(BF16)
    ewg, ewu, ewd = exp_w_gate.astype(BF16), exp_w_up.astype(BF16), exp_w_down.astype(BF16)
    wr = jnp.pad(router_w, ((0, 0), (0, 0), (0, LANES - N_EXPERTS)))

    rope_c, rope_sn, rope_sp = rope_tables(positions.reshape(t, 1))

    x2d = x.reshape(t, d)
    for l in range(depth):
        u_conv, q, k, v = in_proj(x2d, win[l], q_norm_g[l][None], wq[l], kv_norm_g[l][None], wk[l], wv[l],
                                  rope_c, rope_sn, rope_sp)
        y_conv = conv_module(u_conv.reshape(bsz, seq, 2 * d_conv), conv_dw[l], conv_gn_g[l][None], conv_gn_b[l][None])
        y_att = attention(q.reshape(bsz, seq, -1), k.reshape(bsz, seq, -1), v.reshape(bsz, seq, -1))
        x2d = out_proj(y_conv.reshape(t, d_conv), y_att.reshape(t, -1), wo[l, :d_conv], wo[l, d_conv:], x2d,
                       ln_mix_g[l][None], ln_mix_b[l][None])
        j = l // 2
        if l % 2 == 0:
            x2d = dense_ffn(x2d, fwg[j], fwu[j], fwd[j], ln_ffn_g[l][None], ln_ffn_b[l][None])
        else:
            x2d = moe_ffn(x2d, wr[j], ewg[j], ewu[j], ewd[j], ln_ffn_g[l][None], ln_ffn_b[l][None])
    return x2d.reshape(bsz, seq, d)
```

```python
import functools
import math

import jax
import jax.numpy as jnp
from jax import lax
from jax.experimental import pallas as pl
from jax.experimental.pallas import tpu as pltpu

F32 = jnp.float32
BF16 = jnp.bfloat16

CONV_GROUPS = 8
CONV_WIDTH = 31
N_HEADS = 8
QK_NOPE_DIM = 128
QK_ROPE_DIM = 64
V_HEAD_DIM = 128
Q_LORA_RANK = 512
KV_LORA_RANK = 256
ROPE_THETA = 10000.0
N_EXPERTS = 8
DEPTH = 4
ALPHA = (2.0 * DEPTH) ** 0.25
LN_EPS = 1e-5
RMS_EPS = 1e-6

LANES = 128
SUBLANES = 8
VMEM_LIMIT = 56 * 1024 * 1024

HEAD_PAD = 2 * LANES
CONV_HALO = 32

TM_PROJ = 256
TS_CONV = 128
TQ_ATT = 512
HEADS_PER_STEP = 4
TM_OUT = 512
TM_FFN = 1024
TM_HALF = TM_FFN // 2
TF_FFN = 256
TM_ROUTE = 512
TM_COMB = 256
INV_UNROLL = 8
SLAB = 2048 // LANES


def _cparams(sem):
    return pltpu.CompilerParams(dimension_semantics=sem, vmem_limit_bytes=VMEM_LIMIT)


def _const_spec(shape):
    nd = len(shape)
    return pl.BlockSpec(shape, lambda *_: (0,) * nd, pipeline_mode=pl.Buffered(1))


def _layer_norm(z, g, b):
    mu = jnp.mean(z, axis=-1, keepdims=True)
    d = z - mu
    var = jnp.mean(d * d, axis=-1, keepdims=True)
    return d * lax.rsqrt(var + LN_EPS) * g + b


def _rope(x, c, sn, sp):
    return x * c + pltpu.roll(x, LANES - QK_ROPE_DIM // 2, 1) * sn + pltpu.roll(x, QK_ROPE_DIM // 2, 1) * sp


def _rope_table_body(pos_ref, invf_ref, c_ref, sn_ref, sp_ref):
    ang = pos_ref[...].astype(F32) * invf_ref[...]
    c = jnp.cos(ang)
    s = jnp.sin(ang)
    lane = lax.broadcasted_iota(jnp.int32, ang.shape, 1)
    half = QK_ROPE_DIM // 2
    c_ref[...] = jnp.where(lane < QK_ROPE_DIM, c, 0.0)
    sn_ref[...] = jnp.where(lane < half, -s, 0.0)
    sp_ref[...] = jnp.where((lane >= half) & (lane < QK_ROPE_DIM), s, 0.0)


def rope_tables(pos_col):
    t = pos_col.shape[0]
    tm = 1024
    half = QK_ROPE_DIM // 2
    inv_freq = ROPE_THETA ** (-jnp.arange(0, QK_ROPE_DIM, 2, dtype=F32) / QK_ROPE_DIM)
    invf = jnp.concatenate([inv_freq, inv_freq, jnp.zeros((LANES - 2 * half,), F32)]).reshape(1, LANES)
    out = jax.ShapeDtypeStruct((t, LANES), F32)
    row = pl.BlockSpec((tm, LANES), lambda i: (i, 0))
    return pl.pallas_call(
        _rope_table_body,
        grid=(t // tm,),
        in_specs=[pl.BlockSpec((tm, 1), lambda i: (i, 0)), pl.BlockSpec((1, LANES), lambda i: (0, 0))],
        out_specs=[row, row, row],
        out_shape=[out, out, out],
        compiler_params=_cparams(("arbitrary",)),
        name="rope_tables",
    )(pos_col, invf)


def _in_proj_body(x_ref, win_ref, qg_ref, wq_ref, kvg_ref, wk_ref, wv_ref, c_ref, sn_ref, sp_ref,
                  uconv_ref, q_ref, k_ref, v_ref, *, d_conv2, scale):
    xb = x_ref[...].astype(BF16)
    c, sn, sp = c_ref[...], sn_ref[...], sp_ref[...]
    o_q = d_conv2
    o_kv = o_q + Q_LORA_RANK
    o_kr = o_kv + KV_LORA_RANK

    uconv_ref[...] = jnp.dot(xb, win_ref[:, :d_conv2], preferred_element_type=F32)

    uq = jnp.dot(xb, win_ref[:, o_q:o_kv], preferred_element_type=F32)
    ms = jnp.mean(uq * uq, axis=-1, keepdims=True)
    q_lat = (uq * lax.rsqrt(ms + RMS_EPS) * qg_ref[...]).astype(BF16)
    q = jnp.dot(q_lat, wq_ref[...], preferred_element_type=F32) * scale
    for h in range(N_HEADS):
        lo = h * HEAD_PAD
        q_ref[:, lo:lo + LANES] = q[:, lo:lo + LANES].astype(BF16)
        q_ref[:, lo + LANES:lo + HEAD_PAD] = _rope(q[:, lo + LANES:lo + HEAD_PAD], c, sn, sp).astype(BF16)

    ukv = jnp.dot(xb, win_ref[:, o_kv:o_kr], preferred_element_type=F32)
    ms = jnp.mean(ukv * ukv, axis=-1, keepdims=True)
    kv_lat = (ukv * lax.rsqrt(ms + RMS_EPS) * kvg_ref[...]).astype(BF16)
    v = jnp.dot(kv_lat, wv_ref[...], preferred_element_type=F32)
    ones = jnp.ones((v.shape[0], LANES), BF16)
    for h in range(N_HEADS):
        lo = h * HEAD_PAD
        v_ref[:, lo:lo + LANES] = v[:, h * LANES:(h + 1) * LANES].astype(BF16)
        v_ref[:, lo + LANES:lo + HEAD_PAD] = ones
    kn = jnp.dot(kv_lat, wk_ref[...], preferred_element_type=F32)
    ukr = jnp.dot(xb, win_ref[:, o_kr:o_kr + LANES], preferred_element_type=F32)
    kr = _rope(ukr, c, sn, sp).astype(BF16)
    for h in range(N_HEADS):
        lo = h * HEAD_PAD
        k_ref[:, lo:lo + LANES] = kn[:, h * LANES:(h + 1) * LANES].astype(BF16)
        k_ref[:, lo + LANES:lo + HEAD_PAD] = kr


def in_proj(x2d, win, qg, wq, kvg, wk, wv, rope_c, rope_sn, rope_sp):
    t, d = x2d.shape
    tm = TM_PROJ
    d_conv2 = win.shape[1] - (Q_LORA_RANK + KV_LORA_RANK + LANES)
    scale = float((QK_NOPE_DIM + QK_ROPE_DIM) ** -0.5 * math.log2(math.e))
    row = lambda w: pl.BlockSpec((tm, w), lambda i: (i, 0))
    return pl.pallas_call(
        functools.partial(_in_proj_body, d_conv2=d_conv2, scale=scale),
        grid=(t // tm,),
        in_specs=[row(d), _const_spec(win.shape), _const_spec(qg.shape), _const_spec(wq.shape),
                  _const_spec(kvg.shape), _const_spec(wk.shape), _const_spec(wv.shape),
                  row(LANES), row(LANES), row(LANES)],
        out_specs=[row(d_conv2), row(N_HEADS * HEAD_PAD), row(N_HEADS * HEAD_PAD), row(N_HEADS * HEAD_PAD)],
        out_shape=[jax.ShapeDtypeStruct((t, d_conv2), F32),
                   jax.ShapeDtypeStruct((t, N_HEADS * HEAD_PAD), BF16),
                   jax.ShapeDtypeStruct((t, N_HEADS * HEAD_PAD), BF16),
                   jax.ShapeDtypeStruct((t, N_HEADS * HEAD_PAD), BF16)],
        compiler_params=_cparams(("arbitrary",)),
        name="in_proj",
    )(x2d, win, qg, wq, kvg, wk, wv, rope_c, rope_sn, rope_sp)


def _conv_body(a_ref, g_ref, w_ref, gng_ref, gnb_ref, o_ref, hext):
    ts = a_ref.shape[1]
    s = pl.program_id(1)

    @pl.when(s == 0)
    def _():
        hext[0:CONV_HALO, :] = jnp.zeros((CONV_HALO, hext.shape[1]), F32)

    @pl.when(s > 0)
    def _():
        hext[0:CONV_HALO, :] = hext[ts:ts + CONV_HALO, :]

    hext[CONV_HALO:CONV_HALO + ts, :] = a_ref[0] * jax.nn.sigmoid(g_ref[0])

    first = CONV_HALO - (CONV_WIDTH - 1)
    n_groups = hext.shape[1] // LANES
    for c in range(n_groups):
        lanes = slice(c * LANES, (c + 1) * LANES)
        acc = hext[first:first + ts, lanes] * w_ref[0:1, lanes]
        for k in range(1, CONV_WIDTH):
            acc = acc + hext[first + k:first + k + ts, lanes] * w_ref[k:k + 1, lanes]
        mu = jnp.mean(acc, axis=-1, keepdims=True)
        d = acc - mu
        var = jnp.mean(d * d, axis=-1, keepdims=True)
        y = d * lax.rsqrt(var + LN_EPS) * gng_ref[:, lanes] + gnb_ref[:, lanes]
        o_ref[0, :, lanes] = (y * jax.nn.sigmoid(y)).astype(BF16)


def conv_module(u_conv, w_dw, gn_g, gn_b):
    b, s, d2 = u_conv.shape
    dc = d2 // 2
    assert dc // CONV_GROUPS == LANES
    ts = TS_CONV
    return pl.pallas_call(
        _conv_body,
        grid=(b, s // ts),
        in_specs=[pl.BlockSpec((1, ts, dc), lambda i, j: (i, j, 0)),
                  pl.BlockSpec((1, ts, dc), lambda i, j: (i, j, 1)),
                  pl.BlockSpec((CONV_WIDTH, dc), lambda i, j: (0, 0)),
                  pl.BlockSpec((1, dc), lambda i, j: (0, 0)),
                  pl.BlockSpec((1, dc), lambda i, j: (0, 0))],
        out_specs=pl.BlockSpec((1, ts, dc), lambda i, j: (i, j, 0)),
        out_shape=jax.ShapeDtypeStruct((b, s, dc), BF16),
        scratch_shapes=[pltpu.VMEM((CONV_HALO + ts, dc), F32)],
        compiler_params=_cparams(("arbitrary", "arbitrary")),
        name="conv_module",
    )(u_conv, u_conv, w_dw, gn_g, gn_b)


_NT = (((1,), (1,)), ((), ()))


def _attn_body(q_ref, k_ref, v_ref, o_ref, m_sc, acc_sc):
    tq = q_ref.shape[1]
    i = pl.program_id(2)
    m_sc[...] = jnp.full(m_sc.shape, -jnp.inf, F32)
    acc_sc[...] = jnp.zeros(acc_sc.shape, F32)

    def block(j, masked):
        start = pl.multiple_of(j * tq, tq)
        for h in range(HEADS_PER_STEP):
            cols = slice(h * HEAD_PAD, (h + 1) * HEAD_PAD)
            s = lax.dot_general(q_ref[0, :, cols], k_ref[0, pl.ds(start, tq), cols], _NT,
                                preferred_element_type=F32)
            if masked:
                row = lax.broadcasted_iota(jnp.int32, s.shape, 0)
                col = lax.broadcasted_iota(jnp.int32, s.shape, 1)
                s = jnp.where(col <= row, s, -jnp.inf)
            m_prev = m_sc[h]
            m_new = jnp.maximum(m_prev, jnp.max(s, axis=-1, keepdims=True))
            a = jnp.exp2(m_prev - m_new)
            p = jnp.exp2(s - m_new).astype(BF16)
            acc_sc[h] = a * acc_sc[h] + jnp.dot(p, v_ref[0, pl.ds(start, tq), cols], preferred_element_type=F32)
            m_sc[h] = m_new

    def body(j, carry):
        block(j, False)
        return carry

    lax.fori_loop(0, i, body, 0)
    block(i, True)
    for h in range(HEADS_PER_STEP):
        acc = acc_sc[h]
        o_ref[0, :, h * V_HEAD_DIM:(h + 1) * V_HEAD_DIM] = (acc[:, :V_HEAD_DIM] / acc[:, V_HEAD_DIM:]).astype(BF16)


def attention(q, k, v):
    b, s, _ = q.shape
    tq = TQ_ATT
    hp = HEADS_PER_STEP
    return pl.pallas_call(
        _attn_body,
        grid=(b, N_HEADS // hp, s // tq),
        in_specs=[pl.BlockSpec((1, tq, hp * HEAD_PAD), lambda bi, h, i: (bi, i, h)),
                  pl.BlockSpec((1, s, hp * HEAD_PAD), lambda bi, h, i: (bi, 0, h)),
                  pl.BlockSpec((1, s, hp * HEAD_PAD), lambda bi, h, i: (bi, 0, h))],
        out_specs=pl.BlockSpec((1, tq, hp * V_HEAD_DIM), lambda bi, h, i: (bi, i, h)),
        out_shape=jax.ShapeDtypeStruct((b, s, N_HEADS * V_HEAD_DIM), BF16),
        scratch_shapes=[pltpu.VMEM((hp, tq, 1), F32), pltpu.VMEM((hp, tq, HEAD_PAD), F32)],
        compiler_params=_cparams(("arbitrary", "arbitrary", "arbitrary")),
        name="attention",
    )(q, k, v)


def _out_proj_body(yc_ref, ya_ref, wc_ref, wa_ref, x_ref, g_ref, b_ref, o_ref):
    y = jnp.dot(yc_ref[...], wc_ref[...], preferred_element_type=F32)
    y = y + jnp.dot(ya_ref[...], wa_ref[...], preferred_element_type=F32)
    o_ref[...] = _layer_norm(ALPHA * x_ref[...] + y, g_ref[...], b_ref[...])


def out_proj(y_conv, y_att, wo_conv, wo_att, x2d, g, b):
    t, d = x2d.shape
    tm = TM_OUT
    row = lambda w: pl.BlockSpec((tm, w), lambda i: (i, 0))
    return pl.pallas_call(
        _out_proj_body,
        grid=(t // tm,),
        in_specs=[row(y_conv.shape[1]), row(y_att.shape[1]), _const_spec(wo_conv.shape), _const_spec(wo_att.shape),
                  row(d), _const_spec(g.shape), _const_spec(b.shape)],
        out_specs=row(d),
        out_shape=jax.ShapeDtypeStruct((t, d), F32),
        compiler_params=_cparams(("arbitrary",)),
        name="out_proj",
    )(y_conv, y_att, wo_conv, wo_att, x2d, g, b)


def _swiglu_chunk(xb, wg, wu, wd):
    g = jnp.dot(xb, wg, preferred_element_type=F32)
    u = jnp.dot(xb, wu, preferred_element_type=F32)
    h = (g * jax.nn.sigmoid(g) * u).astype(BF16)
    return jnp.dot(h, wd, preferred_element_type=F32)


def _half_rows(hf):
    return slice(hf * TM_HALF, (hf + 1) * TM_HALF)


def _ffn_body(x_ref, wg_ref, wu_ref, wd_ref, g_ref, b_ref, o_ref, xb_sc):
    f = pl.program_id(1)

    @pl.when(f == 0)
    def _():
        xb_sc[...] = x_ref[...].astype(BF16)
        o_ref[...] = jnp.zeros(o_ref.shape, F32)

    wg, wu, wd = wg_ref[...].astype(BF16), wu_ref[...].astype(BF16), wd_ref[...].astype(BF16)
    for hf in range(TM_FFN // TM_HALF):
        rows = _half_rows(hf)
        o_ref[rows, :] += _swiglu_chunk(xb_sc[rows, :], wg, wu, wd)

    @pl.when(f == pl.num_programs(1) - 1)
    def _():
        for hf in range(TM_FFN // TM_HALF):
            rows = _half_rows(hf)
            o_ref[rows, :] = _layer_norm(ALPHA * x_ref[rows, :] + o_ref[rows, :], g_ref[...], b_ref[...])


def dense_ffn(x2d, wg, wu, wd, j, g, b):
    t, d = x2d.shape
    dff = wg.shape[2]
    tm, tf = TM_FFN, TF_FFN
    return pl.pallas_call(
        _ffn_body,
        grid=(t // tm, dff // tf),
        in_specs=[pl.BlockSpec((tm, d), lambda i, f: (i, 0), pipeline_mode=pl.Buffered(1)),
                  pl.BlockSpec((None, d, tf), lambda i, f: (j, 0, f)),
                  pl.BlockSpec((None, d, tf), lambda i, f: (j, 0, f)),
                  pl.BlockSpec((None, tf, d), lambda i, f: (j, f, 0)),
                  pl.BlockSpec((1, d), lambda i, f: (0, 0)),
                  pl.BlockSpec((1, d), lambda i, f: (0, 0))],
        out_specs=pl.BlockSpec((tm, d), lambda i, f: (i, 0)),
        out_shape=jax.ShapeDtypeStruct((t, d), F32),
        scratch_shapes=[pltpu.VMEM((tm, d), BF16)],
        compiler_params=_cparams(("arbitrary", "arbitrary")),
        name="dense_ffn",
    )(x2d, wg, wu, wd, g, b)


_L_IDX, _L_RANK, _L_GATE = 0, 2, 4


def _router_body(x_ref, wr_ref, o_ref, cnt_ref, base_sc):
    tm = x_ref.shape[0]

    @pl.when(pl.program_id(0) == 0)
    def _():
        base_sc[...] = jnp.zeros(base_sc.shape, F32)

    logits = jnp.dot(x_ref[...], wr_ref[...], preferred_element_type=F32, precision=lax.Precision.HIGHEST)
    lane = lax.broadcasted_iota(jnp.int32, logits.shape, 1).astype(F32)
    lg = jnp.where(lane < N_EXPERTS, logits, -jnp.inf)
    m1 = jnp.max(lg, axis=-1, keepdims=True)
    i1 = jnp.min(jnp.where(lg == m1, lane, float(LANES)), axis=-1, keepdims=True)
    oh1 = lane == i1
    lg2 = jnp.where(oh1, -jnp.inf, lg)
    m2 = jnp.max(lg2, axis=-1, keepdims=True)
    i2 = jnp.min(jnp.where(lg2 == m2, lane, float(LANES)), axis=-1, keepdims=True)
    oh2 = lane == i2
    e2 = jnp.exp(m2 - m1)
    g1 = 1.0 / (1.0 + e2)
    g2 = e2 / (1.0 + e2)
    cnt = jnp.where(oh1 | oh2, 1.0, 0.0)
    r = lax.broadcasted_iota(jnp.int32, (tm, tm), 0)
    c = lax.broadcasted_iota(jnp.int32, (tm, tm), 1)
    strict_lower = jnp.where(c < r, 1.0, 0.0).astype(BF16)
    before = jnp.dot(strict_lower, cnt.astype(BF16), preferred_element_type=F32) + base_sc[...]
    r1 = jnp.sum(jnp.where(oh1, before, 0.0), axis=-1, keepdims=True)
    r2 = jnp.sum(jnp.where(oh2, before, 0.0), axis=-1, keepdims=True)
    base_sc[...] += jnp.sum(cnt, axis=0, keepdims=True)
    cnt_ref[...] = base_sc[...]
    packed = jnp.zeros(logits.shape, F32)
    for ln, val in ((_L_IDX, i1), (_L_IDX + 1, i2), (_L_RANK, r1), (_L_RANK + 1, r2), (_L_GATE, g1), (_L_GATE + 1, g2)):
        packed = jnp.where(lane == float(ln), val, packed)
    o_ref[...] = packed


def router(x2d, w_router_pad):
    t, d = x2d.shape
    tm = TM_ROUTE
    return pl.pallas_call(
        _router_body,
        grid=(t // tm,),
        in_specs=[pl.BlockSpec((tm, d), lambda i: (i, 0)), pl.BlockSpec((d, LANES), lambda i: (0, 0))],
        out_specs=[pl.BlockSpec((tm, LANES), lambda i: (i, 0)), pl.BlockSpec((1, LANES), lambda i: (0, 0))],
        out_shape=[jax.ShapeDtypeStruct((t, LANES), F32), jax.ShapeDtypeStruct((1, LANES), F32)],
        scratch_shapes=[pltpu.VMEM((1, LANES), F32)],
        compiler_params=_cparams(("arbitrary",)),
        name="router",
    )(x2d, w_router_pad)


def _invert_body(pos_ref, o_ref):
    def zero(r8, carry):
        for u in range(INV_UNROLL):
            o_ref[r8 * INV_UNROLL + u] = 0
        return carry

    lax.fori_loop(0, o_ref.shape[0] // INV_UNROLL, zero, 0)

    def put(a8, carry):
        for u in range(INV_UNROLL):
            a = a8 * INV_UNROLL + u
            o_ref[pos_ref[a]] = a // 2
        return carry

    lax.fori_loop(0, pos_ref.shape[0] // INV_UNROLL, put, 0)


def invert_positions(pos_flat, n_rows):
    return pl.pallas_call(
        _invert_body,
        in_specs=[pl.BlockSpec(memory_space=pltpu.SMEM)],
        out_specs=pl.BlockSpec(memory_space=pltpu.SMEM),
        out_shape=jax.ShapeDtypeStruct((n_rows,), jnp.int32),
        name="invert_positions",
    )(pos_flat)


def _slab_rows(row):
    start = row * SLAB
    return pl.ds(start if isinstance(start, int) else pl.multiple_of(start, SLAB), SLAB)


def _slab_copy(src_hbm, src_row, dst, dst_row, sem):
    return pltpu.make_async_copy(src_hbm.at[_slab_rows(src_row), :], dst.at[_slab_rows(dst_row), :], sem)


def _start_row_gathers(rt_ref, x_hbm, gbuf, sem, tile, first, count):
    def group(gi, carry):
        for u in range(INV_UNROLL):
            r = first + gi * INV_UNROLL + u
            _slab_copy(x_hbm, rt_ref[tile * TM_FFN + r], gbuf, r, sem).start()
        return carry

    lax.fori_loop(0, count // INV_UNROLL, group, 0)


def _experts_body(te_ref, nv_ref, tr_ref, rt_ref, x_hbm, wg_ref, wu_ref, wd_ref, o_ref,
                  gbuf, xb_sc, acc_sc, sem, *, issue_steps):
    tm = TM_FFN
    i = pl.program_id(0)
    f = pl.program_id(1)
    valid = i < nv_ref[0]

    @pl.when(valid & (f == 0))
    def _():
        @pl.when(i == 0)
        def _():
            _start_row_gathers(rt_ref, x_hbm, gbuf, sem, 0, 0, tm)

        pltpu.make_async_copy(x_hbm.at[pl.ds(0, tm * SLAB), :], gbuf, sem).wait()
        for c in range(SLAB):
            xb_sc[:, c * LANES:(c + 1) * LANES] = gbuf[pl.ds(c, tm, stride=SLAB), :].astype(BF16)
        acc_sc[...] = jnp.zeros(acc_sc.shape, F32)

    @pl.when((i + 1 < nv_ref[0]) & (f >= 1) & (f <= issue_steps))
    def _():
        per_step = tm // issue_steps
        _start_row_gathers(rt_ref, x_hbm, gbuf, sem, i + 1, (f - 1) * per_step, per_step)

    @pl.when(jnp.logical_not(valid) & (f == 0))
    def _():
        o_ref[...] = jnp.zeros(o_ref.shape, F32)

    wg, wu, wd = wg_ref[...].astype(BF16), wu_ref[...].astype(BF16), wd_ref[...].astype(BF16)

    @pl.when(valid)
    def _():
        rows = _half_rows(0)
        acc_sc[rows, :] += _swiglu_chunk(xb_sc[rows, :], wg, wu, wd)

    @pl.when(valid & (tr_ref[i] > TM_HALF))
    def _():
        rows = _half_rows(1)
        acc_sc[rows, :] += _swiglu_chunk(xb_sc[rows, :], wg, wu, wd)

    @pl.when(valid & (f == pl.num_programs(1) - 1))
    def _():
        for c in range(SLAB):
            o_ref[pl.ds(c, tm, stride=SLAB), :] = acc_sc[:, c * LANES:(c + 1) * LANES]


def grouped_experts(tile_expert, n_valid, tile_rows, row_token, x_slab, wg, wu, wd, j):
    n_tiles = tile_expert.shape[0]
    d, dff = wg.shape[2], wg.shape[3]
    tm, tf = TM_FFN, TF_FFN
    nf = dff // tf
    issue_steps = 1
    while issue_steps * 2 <= nf - 1 and tm % (issue_steps * 2 * INV_UNROLL) == 0:
        issue_steps *= 2

    def f_idx(i, f, nv):
        return jnp.where(i < nv[0], f, nf - 1)

    return pl.pallas_call(
        functools.partial(_experts_body, issue_steps=issue_steps),
        grid_spec=pltpu.PrefetchScalarGridSpec(
            num_scalar_prefetch=4,
            grid=(n_tiles, nf),
            in_specs=[pl.BlockSpec(memory_space=pl.ANY),
                      pl.BlockSpec((None, None, d, tf), lambda i, f, te, nv, tr, rt: (j, te[i], 0, f_idx(i, f, nv))),
                      pl.BlockSpec((None, None, d, tf), lambda i, f, te, nv, tr, rt: (j, te[i], 0, f_idx(i, f, nv))),
                      pl.BlockSpec((None, None, tf, d), lambda i, f, te, nv, tr, rt: (j, te[i], f_idx(i, f, nv), 0))],
            out_specs=pl.BlockSpec((tm * SLAB, LANES), lambda i, f, te, nv, tr, rt: (i, 0),
                                   pipeline_mode=pl.Buffered(1)),
            scratch_shapes=[pltpu.VMEM((tm * SLAB, LANES), F32), pltpu.VMEM((tm, d), BF16),
                            pltpu.VMEM((tm, d), F32), pltpu.SemaphoreType.DMA],
        ),
        out_shape=jax.ShapeDtypeStruct((n_tiles * tm * SLAB, LANES), F32),
        compiler_params=_cparams(("arbitrary", "arbitrary")),
        name="grouped_experts",
    )(tile_expert, n_valid, tile_rows, row_token, x_slab, wg, wu, wd)


def _combine_body(pos_ref, y_hbm, x_ref, rt_ref, g_ref, b_ref, o_ref, gbuf, z_sc, sem):
    tm = x_ref.shape[0]
    i = pl.program_id(0)

    def issue(r, carry):
        t = i * tm + r
        _slab_copy(y_hbm, pos_ref[2 * t], gbuf, r, sem).start()
        _slab_copy(y_hbm, pos_ref[2 * t + 1], gbuf, tm + r, sem).start()
        return carry

    lax.fori_loop(0, tm, issue, 0)

    def wait(r, carry):
        _slab_copy(y_hbm, 0, gbuf, r, sem).wait()
        return carry

    lax.fori_loop(0, 2 * tm, wait, 0)
    g1 = rt_ref[:, _L_GATE:_L_GATE + 1]
    g2 = rt_ref[:, _L_GATE + 1:_L_GATE + 2]
    for c in range(SLAB):
        lanes = slice(c * LANES, (c + 1) * LANES)
        y1 = gbuf[pl.ds(c, tm, stride=SLAB), :]
        y2 = gbuf[pl.ds(tm * SLAB + c, tm, stride=SLAB), :]
        z_sc[:, lanes] = ALPHA * x_ref[:, lanes] + (g1 * y1 + g2 * y2)
    o_ref[...] = _layer_norm(z_sc[...], g_ref[...], b_ref[...])


def combine(pos_flat, y_slab, x2d, routed, g, b):
    t, d = x2d.shape
    tm = TM_COMB
    return pl.pallas_call(
        _combine_body,
        grid_spec=pltpu.PrefetchScalarGridSpec(
            num_scalar_prefetch=1,
            grid=(t // tm,),
            in_specs=[pl.BlockSpec(memory_space=pl.ANY),
                      pl.BlockSpec((tm, d), lambda i, pos: (i, 0)),
                      pl.BlockSpec((tm, LANES), lambda i, pos: (i, 0)),
                      pl.BlockSpec((1, d), lambda i, pos: (0, 0)),
                      pl.BlockSpec((1, d), lambda i, pos: (0, 0))],
            out_specs=pl.BlockSpec((tm, d), lambda i, pos: (i, 0)),
            scratch_shapes=[pltpu.VMEM((2 * tm * SLAB, LANES), F32), pltpu.VMEM((tm, d), F32),
                            pltpu.SemaphoreType.DMA],
        ),
        out_shape=jax.ShapeDtypeStruct((t, d), F32),
        compiler_params=_cparams(("arbitrary",)),
        name="moe_combine",
    )(pos_flat, y_slab, x2d, routed, g, b)


def moe_ffn(x2d, w_router_pad, wg, wu, wd, j, g, b):
    t, d = x2d.shape
    tm = TM_FFN
    n_tiles = -(-(t * 2) // tm) + N_EXPERTS
    routed, counts = router(x2d, w_router_pad)
    idx = routed[:, _L_IDX:_L_IDX + 2].astype(jnp.int32)
    rank = routed[:, _L_RANK:_L_RANK + 2].astype(jnp.int32)
    counts = counts[0, :N_EXPERTS].astype(jnp.int32)
    tiles_per_e = (counts + tm - 1) // tm
    tile_end = jnp.cumsum(tiles_per_e)
    tile_start = tile_end - tiles_per_e
    pos = ((tile_start * tm)[idx] + rank).reshape(-1)
    n_valid = tile_end[-1:]
    tile_id = jnp.minimum(jnp.arange(n_tiles, dtype=jnp.int32), n_valid[0] - 1)
    tile_expert = jnp.sum(tile_id[:, None] >= tile_end[None, :], axis=1).astype(jnp.int32)
    tile_rows = jnp.minimum(counts[tile_expert] - (tile_id - tile_start[tile_expert]) * tm, tm).astype(jnp.int32)
    row_token = invert_positions(pos, n_tiles * tm)
    y_slab = grouped_experts(tile_expert, n_valid.astype(jnp.int32), tile_rows, row_token,
                             x2d.reshape(t * SLAB, LANES), wg, wu, wd, j)
    return combine(pos, y_slab, x2d, routed, g, b)


def kernel(x, positions, w_in, conv_dw, conv_gn_g, conv_gn_b, q_norm_g, w_q_b, kv_norm_g, w_kv_b, w_o, ln_mix_g, ln_mix_b, ffn_w_gate, ffn_w_up, ffn_w_down, router_w, exp_w_gate, exp_w_up, exp_w_down, ln_ffn_g, ln_ffn_b):
    bsz, seq, d = x.shape
    assert d == SLAB * LANES
    t = bsz * seq
    depth = w_in.shape[0]
    d_conv = conv_dw.shape[2]

    win = jnp.pad(w_in, ((0, 0), (0, 0), (0, LANES - QK_ROPE_DIM))).astype(BF16)
    wq = w_q_b.reshape(depth, Q_LORA_RANK, N_HEADS, QK_NOPE_DIM + QK_ROPE_DIM)
    wq = jnp.pad(wq, ((0, 0), (0, 0), (0, 0), (0, HEAD_PAD - QK_NOPE_DIM - QK_ROPE_DIM)))
    wq = wq.reshape(depth, Q_LORA_RANK, N_HEADS * HEAD_PAD).astype(BF16)
    wkv = w_kv_b.reshape(depth, KV_LORA_RANK, N_HEADS, QK_NOPE_DIM + V_HEAD_DIM)
    wk = wkv[..., :QK_NOPE_DIM].reshape(depth, KV_LORA_RANK, N_HEADS * QK_NOPE_DIM).astype(BF16)
    wv = wkv[..., QK_NOPE_DIM:].reshape(depth, KV_LORA_RANK, N_HEADS * V_HEAD_DIM).astype(BF16)
    wo = w_o.astype(BF16)
    wr = jnp.pad(router_w, ((0, 0), (0, 0), (0, LANES - N_EXPERTS)))

    rope_c, rope_sn, rope_sp = rope_tables(positions.reshape(t, 1))

    x2d = x.reshape(t, d)
    for l in range(depth):
        u_conv, q, k, v = in_proj(x2d, win[l], q_norm_g[l][None], wq[l], kv_norm_g[l][None], wk[l], wv[l],
                                  rope_c, rope_sn, rope_sp)
        y_conv = conv_module(u_conv.reshape(bsz, seq, 2 * d_conv), conv_dw[l], conv_gn_g[l][None], conv_gn_b[l][None])
        y_att = attention(q.reshape(bsz, seq, -1), k.reshape(bsz, seq, -1), v.reshape(bsz, seq, -1))
        x2d = out_proj(y_conv.reshape(t, d_conv), y_att.reshape(t, -1), wo[l, :d_conv], wo[l, d_conv:], x2d,
                       ln_mix_g[l][None], ln_mix_b[l][None])
        j = l // 2
        if l % 2 == 0:
            x2d = dense_ffn(x2d, ffn_w_gate, ffn_w_up, ffn_w_down, j, ln_ffn_g[l][None], ln_ffn_b[l][None])
        else:
            x2d = moe_ffn(x2d, wr[j], exp_w_gate, exp_w_up, exp_w_down, j, ln_ffn_g[l][None], ln_ffn_b[l][None])
    return x2d.reshape(bsz, seq, d)
```

```python
import functools
import math

import jax
import jax.numpy as jnp
from jax import lax
from jax.experimental import pallas as pl
from jax.experimental.pallas import tpu as pltpu

F32 = jnp.float32
BF16 = jnp.bfloat16

CONV_GROUPS = 8
CONV_WIDTH = 31
N_HEADS = 8
QK_NOPE_DIM = 128
QK_ROPE_DIM = 64
V_HEAD_DIM = 128
Q_LORA_RANK = 512
KV_LORA_RANK = 256
ROPE_THETA = 10000.0
N_EXPERTS = 8
DEPTH = 4
ALPHA = (2.0 * DEPTH) ** 0.25
LN_EPS = 1e-5
RMS_EPS = 1e-6

LANES = 128
SUBLANES = 8
VMEM_LIMIT = 56 * 1024 * 1024

HEAD_PAD = 2 * LANES
CONV_HALO = 32

TM_PROJ = 256
TS_CONV = 128
TQ_ATT = 512
HEADS_PER_STEP = 4
TM_OUT = 512
TM_FFN = 1024
TM_HALF = TM_FFN // 2
TF_FFN = 256
TM_ROUTE = 512
TM_COMB = 256
INV_UNROLL = 8
SLAB = 2048 // LANES


def _cparams(sem):
    return pltpu.CompilerParams(dimension_semantics=sem, vmem_limit_bytes=VMEM_LIMIT)


def _const_spec(shape):
    nd = len(shape)
    return pl.BlockSpec(shape, lambda *_: (0,) * nd, pipeline_mode=pl.Buffered(1))


def _layer_norm(z, g, b):
    mu = jnp.mean(z, axis=-1, keepdims=True)
    d = z - mu
    var = jnp.mean(d * d, axis=-1, keepdims=True)
    return d * lax.rsqrt(var + LN_EPS) * g + b


def _rope(x, c, sn, sp):
    return x * c + pltpu.roll(x, LANES - QK_ROPE_DIM // 2, 1) * sn + pltpu.roll(x, QK_ROPE_DIM // 2, 1) * sp


def _rope_table_body(pos_ref, invf_ref, c_ref, sn_ref, sp_ref):
    ang = pos_ref[...].astype(F32) * invf_ref[...]
    c = jnp.cos(ang)
    s = jnp.sin(ang)
    lane = lax.broadcasted_iota(jnp.int32, ang.shape, 1)
    half = QK_ROPE_DIM // 2
    c_ref[...] = jnp.where(lane < QK_ROPE_DIM, c, 0.0)
    sn_ref[...] = jnp.where(lane < half, -s, 0.0)
    sp_ref[...] = jnp.where((lane >= half) & (lane < QK_ROPE_DIM), s, 0.0)


def rope_tables(pos_col):
    t = pos_col.shape[0]
    tm = 1024
    half = QK_ROPE_DIM // 2
    inv_freq = ROPE_THETA ** (-jnp.arange(0, QK_ROPE_DIM, 2, dtype=F32) / QK_ROPE_DIM)
    invf = jnp.concatenate([inv_freq, inv_freq, jnp.zeros((LANES - 2 * half,), F32)]).reshape(1, LANES)
    out = jax.ShapeDtypeStruct((t, LANES), F32)
    row = pl.BlockSpec((tm, LANES), lambda i: (i, 0))
    return pl.pallas_call(
        _rope_table_body,
        grid=(t // tm,),
        in_specs=[pl.BlockSpec((tm, 1), lambda i: (i, 0)), pl.BlockSpec((1, LANES), lambda i: (0, 0))],
        out_specs=[row, row, row],
        out_shape=[out, out, out],
        compiler_params=_cparams(("arbitrary",)),
        name="rope_tables",
    )(pos_col, invf)


def _in_proj_body(x_ref, win_ref, qg_ref, wq_ref, kvg_ref, wk_ref, wv_ref, c_ref, sn_ref, sp_ref,
                  uconv_ref, q_ref, k_ref, v_ref, *, d_conv2, scale):
    xb = x_ref[...].astype(BF16)
    c, sn, sp = c_ref[...], sn_ref[...], sp_ref[...]
    o_q = d_conv2
    o_kv = o_q + Q_LORA_RANK
    o_kr = o_kv + KV_LORA_RANK

    uconv_ref[...] = jnp.dot(xb, win_ref[:, :d_conv2], preferred_element_type=F32)

    uq = jnp.dot(xb, win_ref[:, o_q:o_kv], preferred_element_type=F32)
    ms = jnp.mean(uq * uq, axis=-1, keepdims=True)
    q_lat = (uq * lax.rsqrt(ms + RMS_EPS) * qg_ref[...]).astype(BF16)
    q = jnp.dot(q_lat, wq_ref[...], preferred_element_type=F32) * scale
    for h in range(N_HEADS):
        lo = h * HEAD_PAD
        q_ref[:, lo:lo + LANES] = q[:, lo:lo + LANES].astype(BF16)
        q_ref[:, lo + LANES:lo + HEAD_PAD] = _rope(q[:, lo + LANES:lo + HEAD_PAD], c, sn, sp).astype(BF16)

    ukv = jnp.dot(xb, win_ref[:, o_kv:o_kr], preferred_element_type=F32)
    ms = jnp.mean(ukv * ukv, axis=-1, keepdims=True)
    kv_lat = (ukv * lax.rsqrt(ms + RMS_EPS) * kvg_ref[...]).astype(BF16)
    v = jnp.dot(kv_lat, wv_ref[...], preferred_element_type=F32)
    ones = jnp.ones((v.shape[0], LANES), BF16)
    for h in range(N_HEADS):
        lo = h * HEAD_PAD
        v_ref[:, lo:lo + LANES] = v[:, h * LANES:(h + 1) * LANES].astype(BF16)
        v_ref[:, lo + LANES:lo + HEAD_PAD] = ones
    kn = jnp.dot(kv_lat, wk_ref[...], preferred_element_type=F32)
    ukr = jnp.dot(xb, win_ref[:, o_kr:o_kr + LANES], preferred_element_type=F32)
    kr = _rope(ukr, c, sn, sp).astype(BF16)
    for h in range(N_HEADS):
        lo = h * HEAD_PAD
        k_ref[:, lo:lo + LANES] = kn[:, h * LANES:(h + 1) * LANES].astype(BF16)
        k_ref[:, lo + LANES:lo + HEAD_PAD] = kr


def in_proj(x2d, win, qg, wq, kvg, wk, wv, rope_c, rope_sn, rope_sp):
    t, d = x2d.shape
    tm = TM_PROJ
    d_conv2 = win.shape[1] - (Q_LORA_RANK + KV_LORA_RANK + LANES)
    scale = float((QK_NOPE_DIM + QK_ROPE_DIM) ** -0.5 * math.log2(math.e))
    row = lambda w: pl.BlockSpec((tm, w), lambda i: (i, 0))
    return pl.pallas_call(
        functools.partial(_in_proj_body, d_conv2=d_conv2, scale=scale),
        grid=(t // tm,),
        in_specs=[row(d), _const_spec(win.shape), _const_spec(qg.shape), _const_spec(wq.shape),
                  _const_spec(kvg.shape), _const_spec(wk.shape), _const_spec(wv.shape),
                  row(LANES), row(LANES), row(LANES)],
        out_specs=[row(d_conv2), row(N_HEADS * HEAD_PAD), row(N_HEADS * HEAD_PAD), row(N_HEADS * HEAD_PAD)],
        out_shape=[jax.ShapeDtypeStruct((t, d_conv2), F32),
                   jax.ShapeDtypeStruct((t, N_HEADS * HEAD_PAD), BF16),
                   jax.ShapeDtypeStruct((t, N_HEADS * HEAD_PAD), BF16),
                   jax.ShapeDtypeStruct((t, N_HEADS * HEAD_PAD), BF16)],
        compiler_params=_cparams(("arbitrary",)),
        name="in_proj",
    )(x2d, win, qg, wq, kvg, wk, wv, rope_c, rope_sn, rope_sp)


def _conv_body(a_ref, g_ref, w_ref, gng_ref, gnb_ref, o_ref, hext, shifted):
    ts = a_ref.shape[1]
    s = pl.program_id(1)

    @pl.when(s == 0)
    def _():
        hext[0:CONV_HALO, :] = jnp.zeros((CONV_HALO, hext.shape[1]), F32)

    @pl.when(s > 0)
    def _():
        hext[0:CONV_HALO, :] = hext[ts:ts + CONV_HALO, :]

    hext[CONV_HALO:CONV_HALO + ts, :] = a_ref[0] * jax.nn.sigmoid(g_ref[0])

    first = CONV_HALO - (CONV_WIDTH - 1)
    n_groups = hext.shape[1] // LANES
    for c in range(n_groups):
        lanes = slice(c * LANES, (c + 1) * LANES)
        acc = None
        for b in range(SUBLANES):
            offs = [o for o in range(first, first + CONV_WIDTH) if o % SUBLANES == b]
            if b:
                n = offs[-1] - b + ts
                shifted[b, 0:n, :] = hext[b:b + n, lanes]
            for o in offs:
                hb = shifted[b, o - b:o - b + ts, :] if b else hext[o:o + ts, lanes]
                term = hb * w_ref[o - first:o - first + 1, lanes]
                acc = term if acc is None else acc + term
        mu = jnp.mean(acc, axis=-1, keepdims=True)
        d = acc - mu
        var = jnp.mean(d * d, axis=-1, keepdims=True)
        y = d * lax.rsqrt(var + LN_EPS) * gng_ref[:, lanes] + gnb_ref[:, lanes]
        o_ref[0, :, lanes] = (y * jax.nn.sigmoid(y)).astype(BF16)


def conv_module(u_conv, w_dw, gn_g, gn_b):
    b, s, d2 = u_conv.shape
    dc = d2 // 2
    assert dc // CONV_GROUPS == LANES
    ts = TS_CONV
    return pl.pallas_call(
        _conv_body,
        grid=(b, s // ts),
        in_specs=[pl.BlockSpec((1, ts, dc), lambda i, j: (i, j, 0)),
                  pl.BlockSpec((1, ts, dc), lambda i, j: (i, j, 1)),
                  pl.BlockSpec((CONV_WIDTH, dc), lambda i, j: (0, 0)),
                  pl.BlockSpec((1, dc), lambda i, j: (0, 0)),
                  pl.BlockSpec((1, dc), lambda i, j: (0, 0))],
        out_specs=pl.BlockSpec((1, ts, dc), lambda i, j: (i, j, 0)),
        out_shape=jax.ShapeDtypeStruct((b, s, dc), BF16),
        scratch_shapes=[pltpu.VMEM((CONV_HALO + ts, dc), F32), pltpu.VMEM((SUBLANES, CONV_HALO + ts, LANES), F32)],
        compiler_params=_cparams(("arbitrary", "arbitrary")),
        name="conv_module",
    )(u_conv, u_conv, w_dw, gn_g, gn_b)


_NT = (((1,), (1,)), ((), ()))


def _attn_body(q_ref, k_ref, v_ref, o_ref, m_sc, acc_sc):
    tq = q_ref.shape[1]
    i = pl.program_id(2)
    m_sc[...] = jnp.full(m_sc.shape, -jnp.inf, F32)
    acc_sc[...] = jnp.zeros(acc_sc.shape, F32)

    def block(j, masked):
        start = pl.multiple_of(j * tq, tq)
        for h in range(HEADS_PER_STEP):
            cols = slice(h * HEAD_PAD, (h + 1) * HEAD_PAD)
            s = lax.dot_general(q_ref[0, :, cols], k_ref[0, pl.ds(start, tq), cols], _NT,
                                preferred_element_type=F32)
            if masked:
                row = lax.broadcasted_iota(jnp.int32, s.shape, 0)
                col = lax.broadcasted_iota(jnp.int32, s.shape, 1)
                s = jnp.where(col <= row, s, -jnp.inf)
            m_prev = m_sc[h]
            m_new = jnp.maximum(m_prev, jnp.max(s, axis=-1, keepdims=True))
            a = jnp.exp2(m_prev - m_new)
            p = jnp.exp2(s - m_new).astype(BF16)
            acc_sc[h] = a * acc_sc[h] + jnp.dot(p, v_ref[0, pl.ds(start, tq), cols], preferred_element_type=F32)
            m_sc[h] = m_new

    def body(j, carry):
        block(j, False)
        return carry

    lax.fori_loop(0, i, body, 0)
    block(i, True)
    for h in range(HEADS_PER_STEP):
        acc = acc_sc[h]
        o_ref[0, :, h * V_HEAD_DIM:(h + 1) * V_HEAD_DIM] = (acc[:, :V_HEAD_DIM] / acc[:, V_HEAD_DIM:]).astype(BF16)


def attention(q, k, v):
    b, s, _ = q.shape
    tq = TQ_ATT
    hp = HEADS_PER_STEP
    return pl.pallas_call(
        _attn_body,
        grid=(b, N_HEADS // hp, s // tq),
        in_specs=[pl.BlockSpec((1, tq, hp * HEAD_PAD), lambda bi, h, i: (bi, i, h)),
                  pl.BlockSpec((1, s, hp * HEAD_PAD), lambda bi, h, i: (bi, 0, h)),
                  pl.BlockSpec((1, s, hp * HEAD_PAD), lambda bi, h, i: (bi, 0, h))],
        out_specs=pl.BlockSpec((1, tq, hp * V_HEAD_DIM), lambda bi, h, i: (bi, i, h)),
        out_shape=jax.ShapeDtypeStruct((b, s, N_HEADS * V_HEAD_DIM), BF16),
        scratch_shapes=[pltpu.VMEM((hp, tq, 1), F32), pltpu.VMEM((hp, tq, HEAD_PAD), F32)],
        compiler_params=_cparams(("arbitrary", "arbitrary", "arbitrary")),
        name="attention",
    )(q, k, v)


def _out_proj_body(yc_ref, ya_ref, wc_ref, wa_ref, x_ref, g_ref, b_ref, o_ref, *maybe_slab_ref):
    y = jnp.dot(yc_ref[...], wc_ref[...], preferred_element_type=F32)
    y = y + jnp.dot(ya_ref[...], wa_ref[...], preferred_element_type=F32)
    o_ref[...] = _layer_norm(ALPHA * x_ref[...] + y, g_ref[...], b_ref[...])
    for slab_ref in maybe_slab_ref:
        tm = o_ref.shape[0]
        for c in range(SLAB):
            slab_ref[pl.ds(c, tm, stride=SLAB), :] = o_ref[:, c * LANES:(c + 1) * LANES]


def out_proj(y_conv, y_att, wo_conv, wo_att, x2d, g, b, with_slab):
    t, d = x2d.shape
    tm = TM_OUT
    row = lambda w: pl.BlockSpec((tm, w), lambda i: (i, 0))
    out_specs = [row(d)]
    out_shape = [jax.ShapeDtypeStruct((t, d), F32)]
    if with_slab:
        out_specs.append(pl.BlockSpec((tm * SLAB, LANES), lambda i: (i, 0)))
        out_shape.append(jax.ShapeDtypeStruct((t * SLAB, LANES), F32))
    return pl.pallas_call(
        _out_proj_body,
        grid=(t // tm,),
        in_specs=[row(y_conv.shape[1]), row(y_att.shape[1]), _const_spec(wo_conv.shape), _const_spec(wo_att.shape),
                  row(d), _const_spec(g.shape), _const_spec(b.shape)],
        out_specs=out_specs,
        out_shape=out_shape,
        compiler_params=_cparams(("arbitrary",)),
        name="out_proj",
    )(y_conv, y_att, wo_conv, wo_att, x2d, g, b)


def _swiglu_chunk(xb, wg, wu, wd):
    g = jnp.dot(xb, wg, preferred_element_type=F32)
    u = jnp.dot(xb, wu, preferred_element_type=F32)
    h = (g * jax.nn.sigmoid(g) * u).astype(BF16)
    return jnp.dot(h, wd, preferred_element_type=F32)


def _half_rows(hf):
    return slice(hf * TM_HALF, (hf + 1) * TM_HALF)


def _ffn_body(x_ref, wg_ref, wu_ref, wd_ref, g_ref, b_ref, o_ref, xb_sc):
    f = pl.program_id(1)

    @pl.when(f == 0)
    def _():
        xb_sc[...] = x_ref[...].astype(BF16)
        o_ref[...] = jnp.zeros(o_ref.shape, F32)

    wg, wu, wd = wg_ref[...].astype(BF16), wu_ref[...].astype(BF16), wd_ref[...].astype(BF16)
    for hf in range(TM_FFN // TM_HALF):
        rows = _half_rows(hf)
        o_ref[rows, :] += _swiglu_chunk(xb_sc[rows, :], wg, wu, wd)

    @pl.when(f == pl.num_programs(1) - 1)
    def _():
        for hf in range(TM_FFN // TM_HALF):
            rows = _half_rows(hf)
            o_ref[rows, :] = _layer_norm(ALPHA * x_ref[rows, :] + o_ref[rows, :], g_ref[...], b_ref[...])


def dense_ffn(x2d, wg, wu, wd, j, g, b):
    t, d = x2d.shape
    dff = wg.shape[2]
    tm, tf = TM_FFN, TF_FFN
    return pl.pallas_call(
        _ffn_body,
        grid=(t // tm, dff // tf),
        in_specs=[pl.BlockSpec((tm, d), lambda i, f: (i, 0), pipeline_mode=pl.Buffered(1)),
                  pl.BlockSpec((None, d, tf), lambda i, f: (j, 0, f)),
                  pl.BlockSpec((None, d, tf), lambda i, f: (j, 0, f)),
                  pl.BlockSpec((None, tf, d), lambda i, f: (j, f, 0)),
                  pl.BlockSpec((1, d), lambda i, f: (0, 0)),
                  pl.BlockSpec((1, d), lambda i, f: (0, 0))],
        out_specs=pl.BlockSpec((tm, d), lambda i, f: (i, 0)),
        out_shape=jax.ShapeDtypeStruct((t, d), F32),
        scratch_shapes=[pltpu.VMEM((tm, d), BF16)],
        compiler_params=_cparams(("arbitrary", "arbitrary")),
        name="dense_ffn",
    )(x2d, wg, wu, wd, g, b)


_L_IDX, _L_RANK, _L_GATE = 0, 2, 4


def _router_body(x_ref, wr_ref, o_ref, cnt_ref, base_sc):
    tm = x_ref.shape[0]

    @pl.when(pl.program_id(0) == 0)
    def _():
        base_sc[...] = jnp.zeros(base_sc.shape, F32)

    logits = jnp.dot(x_ref[...], wr_ref[...], preferred_element_type=F32, precision=lax.Precision.HIGHEST)
    lane = lax.broadcasted_iota(jnp.int32, logits.shape, 1).astype(F32)
    lg = jnp.where(lane < N_EXPERTS, logits, -jnp.inf)
    m1 = jnp.max(lg, axis=-1, keepdims=True)
    i1 = jnp.min(jnp.where(lg == m1, lane, float(LANES)), axis=-1, keepdims=True)
    oh1 = lane == i1
    lg2 = jnp.where(oh1, -jnp.inf, lg)
    m2 = jnp.max(lg2, axis=-1, keepdims=True)
    i2 = jnp.min(jnp.where(lg2 == m2, lane, float(LANES)), axis=-1, keepdims=True)
    oh2 = lane == i2
    e2 = jnp.exp(m2 - m1)
    g1 = 1.0 / (1.0 + e2)
    g2 = e2 / (1.0 + e2)
    cnt = jnp.where(oh1 | oh2, 1.0, 0.0)
    r = lax.broadcasted_iota(jnp.int32, (tm, tm), 0)
    c = lax.broadcasted_iota(jnp.int32, (tm, tm), 1)
    strict_lower = jnp.where(c < r, 1.0, 0.0).astype(BF16)
    before = jnp.dot(strict_lower, cnt.astype(BF16), preferred_element_type=F32) + base_sc[...]
    r1 = jnp.sum(jnp.where(oh1, before, 0.0), axis=-1, keepdims=True)
    r2 = jnp.sum(jnp.where(oh2, before, 0.0), axis=-1, keepdims=True)
    base_sc[...] += jnp.sum(cnt, axis=0, keepdims=True)
    cnt_ref[...] = base_sc[...]
    packed = jnp.zeros(logits.shape, F32)
    for ln, val in ((_L_IDX, i1), (_L_IDX + 1, i2), (_L_RANK, r1), (_L_RANK + 1, r2), (_L_GATE, g1), (_L_GATE + 1, g2)):
        packed = jnp.where(lane == float(ln), val, packed)
    o_ref[...] = packed


def router(x2d, w_router_pad):
    t, d = x2d.shape
    tm = TM_ROUTE
    return pl.pallas_call(
        _router_body,
        grid=(t // tm,),
        in_specs=[pl.BlockSpec((tm, d), lambda i: (i, 0)), pl.BlockSpec((d, LANES), lambda i: (0, 0))],
        out_specs=[pl.BlockSpec((tm, LANES), lambda i: (i, 0)), pl.BlockSpec((1, LANES), lambda i: (0, 0))],
        out_shape=[jax.ShapeDtypeStruct((t, LANES), F32), jax.ShapeDtypeStruct((1, LANES), F32)],
        scratch_shapes=[pltpu.VMEM((1, LANES), F32)],
        compiler_params=_cparams(("arbitrary",)),
        name="router",
    )(x2d, w_router_pad)


def _invert_body(pos_ref, o_ref):
    def zero(r8, carry):
        for u in range(INV_UNROLL):
            o_ref[r8 * INV_UNROLL + u] = 0
        return carry

    lax.fori_loop(0, o_ref.shape[0] // INV_UNROLL, zero, 0)

    def put(a8, carry):
        for u in range(INV_UNROLL):
            a = a8 * INV_UNROLL + u
            o_ref[pos_ref[a]] = lax.shift_right_logical(a, 1)
        return carry

    lax.fori_loop(0, pos_ref.shape[0] // INV_UNROLL, put, 0)


def invert_positions(pos_flat, n_rows):
    return pl.pallas_call(
        _invert_body,
        in_specs=[pl.BlockSpec(memory_space=pltpu.SMEM)],
        out_specs=pl.BlockSpec(memory_space=pltpu.SMEM),
        out_shape=jax.ShapeDtypeStruct((n_rows,), jnp.int32),
        name="invert_positions",
    )(pos_flat)


def _slab_rows(row):
    start = row * SLAB
    return pl.ds(start if isinstance(start, int) else pl.multiple_of(start, SLAB), SLAB)


def _slab_copy(src_hbm, src_row, dst, dst_row, sem):
    return pltpu.make_async_copy(src_hbm.at[_slab_rows(src_row), :], dst.at[_slab_rows(dst_row), :], sem)


def _experts_body(te_ref, nv_ref, tr_ref, rt_ref, x_hbm, wg_ref, wu_ref, wd_ref, o_ref,
                  gbuf, xb_sc, acc_sc, sem, *, per_step):
    tm = TM_FFN
    i = pl.program_id(0)
    f = pl.program_id(1)
    nf = pl.num_programs(1)
    valid = i < nv_ref[0]
    last_row = rt_ref.shape[0] - 1

    def start_gather(tile, slot):
        token = rt_ref[jnp.minimum(tile * tm + slot, last_row)]
        _slab_copy(x_hbm, token, gbuf, slot, sem).start()

    def wait_gathers():
        pltpu.make_async_copy(x_hbm.at[pl.ds(0, gbuf.shape[0]), :], gbuf, sem).wait()

    @pl.when(valid & (f == 0))
    def _():
        @pl.when(i == 0)
        def _():
            def first_tile(slot, carry):
                start_gather(0, slot)
                return carry

            lax.fori_loop(0, gbuf.shape[0] // SLAB, first_tile, 0)

        wait_gathers()
        for c in range(SLAB):
            xb_sc[:, c * LANES:(c + 1) * LANES] = gbuf[pl.ds(c, tm, stride=SLAB), :].astype(BF16)
        acc_sc[...] = jnp.zeros(acc_sc.shape, F32)

    @pl.when(jnp.logical_not(valid) & (f == 0))
    def _():
        o_ref[...] = jnp.zeros(o_ref.shape, F32)

    @pl.when(valid)
    def _():
        nxt = jnp.minimum(i + 1, pl.num_programs(0) - 1)
        for u in range(per_step):
            start_gather(nxt, f * per_step + u)
        wg, wu, wd = wg_ref[...].astype(BF16), wu_ref[...].astype(BF16), wd_ref[...].astype(BF16)
        rows = _half_rows(0)
        acc_sc[rows, :] += _swiglu_chunk(xb_sc[rows, :], wg, wu, wd)

        @pl.when(tr_ref[i] > TM_HALF)
        def _():
            rows = _half_rows(1)
            acc_sc[rows, :] += _swiglu_chunk(xb_sc[rows, :], wg, wu, wd)

    @pl.when(valid & (f == nf - 1))
    def _():
        for c in range(SLAB):
            o_ref[pl.ds(c, tm, stride=SLAB), :] = acc_sc[:, c * LANES:(c + 1) * LANES]

        @pl.when(i == nv_ref[0] - 1)
        def _():
            wait_gathers()


def grouped_experts(tile_expert, n_valid, tile_rows, row_token, x_slab, wg, wu, wd, j):
    n_tiles = tile_expert.shape[0]
    d, dff = wg.shape[2], wg.shape[3]
    tm, tf = TM_FFN, TF_FFN
    nf = dff // tf
    per_step = -(-tm // nf)

    def f_idx(i, f, nv):
        return jnp.where(i < nv[0], f, nf - 1)

    return pl.pallas_call(
        functools.partial(_experts_body, per_step=per_step),
        grid_spec=pltpu.PrefetchScalarGridSpec(
            num_scalar_prefetch=4,
            grid=(n_tiles, nf),
            in_specs=[pl.BlockSpec(memory_space=pl.ANY),
                      pl.BlockSpec((None, None, d, tf), lambda i, f, te, nv, tr, rt: (j, te[i], 0, f_idx(i, f, nv))),
                      pl.BlockSpec((None, None, d, tf), lambda i, f, te, nv, tr, rt: (j, te[i], 0, f_idx(i, f, nv))),
                      pl.BlockSpec((None, None, tf, d), lambda i, f, te, nv, tr, rt: (j, te[i], f_idx(i, f, nv), 0))],
            out_specs=pl.BlockSpec((tm * SLAB, LANES), lambda i, f, te, nv, tr, rt: (i, 0),
                                   pipeline_mode=pl.Buffered(1)),
            scratch_shapes=[pltpu.VMEM((nf * per_step * SLAB, LANES), F32), pltpu.VMEM((tm, d), BF16),
                            pltpu.VMEM((tm, d), F32), pltpu.SemaphoreType.DMA],
        ),
        out_shape=jax.ShapeDtypeStruct((n_tiles * tm * SLAB, LANES), F32),
        compiler_params=_cparams(("arbitrary", "arbitrary")),
        name="grouped_experts",
    )(tile_expert, n_valid, tile_rows, row_token, x_slab, wg, wu, wd)


def _combine_body(pos_ref, y_hbm, x_ref, rt_ref, g_ref, b_ref, o_ref, gbuf_a, gbuf_b, z_sc, sem_a, sem_b):
    tm = TM_COMB
    i = pl.program_id(0)

    def start_gathers(sub_tile, gbuf, sem):
        def group(gi, carry):
            for u in range(INV_UNROLL):
                r = gi * INV_UNROLL + u
                t = sub_tile * tm + r
                _slab_copy(y_hbm, pos_ref[2 * t], gbuf, r, sem).start()
                _slab_copy(y_hbm, pos_ref[2 * t + 1], gbuf, tm + r, sem).start()
            return carry

        lax.fori_loop(0, tm // INV_UNROLL, group, 0)

    def finish(half, gbuf, sem):
        pltpu.make_async_copy(y_hbm.at[pl.ds(0, gbuf.shape[0]), :], gbuf, sem).wait()
        rows = slice(half * tm, (half + 1) * tm)
        g1 = rt_ref[rows, _L_GATE:_L_GATE + 1]
        g2 = rt_ref[rows, _L_GATE + 1:_L_GATE + 2]
        for c in range(SLAB):
            lanes = slice(c * LANES, (c + 1) * LANES)
            y1 = gbuf[pl.ds(c, tm, stride=SLAB), :]
            y2 = gbuf[pl.ds(tm * SLAB + c, tm, stride=SLAB), :]
            z_sc[:, lanes] = ALPHA * x_ref[rows, lanes] + (g1 * y1 + g2 * y2)
        o_ref[rows, :] = _layer_norm(z_sc[...], g_ref[...], b_ref[...])

    @pl.when(i == 0)
    def _():
        start_gathers(0, gbuf_a, sem_a)

    start_gathers(2 * i + 1, gbuf_b, sem_b)
    finish(0, gbuf_a, sem_a)

    @pl.when(i + 1 < pl.num_programs(0))
    def _():
        start_gathers(2 * i + 2, gbuf_a, sem_a)

    finish(1, gbuf_b, sem_b)


def combine(pos_flat, y_slab, x2d, routed, g, b):
    t, d = x2d.shape
    tm = 2 * TM_COMB
    gbuf = pltpu.VMEM((2 * TM_COMB * SLAB, LANES), F32)
    return pl.pallas_call(
        _combine_body,
        grid_spec=pltpu.PrefetchScalarGridSpec(
            num_scalar_prefetch=1,
            grid=(t // tm,),
            in_specs=[pl.BlockSpec(memory_space=pl.ANY),
                      pl.BlockSpec((tm, d), lambda i, pos: (i, 0)),
                      pl.BlockSpec((tm, LANES), lambda i, pos: (i, 0)),
                      pl.BlockSpec((1, d), lambda i, pos: (0, 0)),
                      pl.BlockSpec((1, d), lambda i, pos: (0, 0))],
            out_specs=pl.BlockSpec((tm, d), lambda i, pos: (i, 0)),
            scratch_shapes=[gbuf, gbuf, pltpu.VMEM((TM_COMB, d), F32),
                            pltpu.SemaphoreType.DMA, pltpu.SemaphoreType.DMA],
        ),
        out_shape=jax.ShapeDtypeStruct((t, d), F32),
        compiler_params=_cparams(("arbitrary",)),
        name="moe_combine",
    )(pos_flat, y_slab, x2d, routed, g, b)


def moe_ffn(x2d, x_slab, w_router_pad, wg, wu, wd, j, g, b):
    t, d = x2d.shape
    tm = TM_FFN
    n_tiles = -(-(t * 2) // tm) + N_EXPERTS
    routed, counts = router(x2d, w_router_pad)
    idx = routed[:, _L_IDX:_L_IDX + 2].astype(jnp.int32)
    rank = routed[:, _L_RANK:_L_RANK + 2].astype(jnp.int32)
    counts = counts[0, :N_EXPERTS].astype(jnp.int32)
    tiles_per_e = (counts + tm - 1) // tm
    tile_end = jnp.cumsum(tiles_per_e)
    tile_start = tile_end - tiles_per_e
    pos = ((tile_start * tm)[idx] + rank).reshape(-1)
    n_valid = tile_end[-1:]
    tile_id = jnp.minimum(jnp.arange(n_tiles, dtype=jnp.int32), n_valid[0] - 1)
    tile_expert = jnp.sum(tile_id[:, None] >= tile_end[None, :], axis=1).astype(jnp.int32)
    tile_rows = jnp.minimum(counts[tile_expert] - (tile_id - tile_start[tile_expert]) * tm, tm).astype(jnp.int32)
    row_token = invert_positions(pos, n_tiles * tm)
    y_slab = grouped_experts(tile_expert, n_valid.astype(jnp.int32), tile_rows, row_token, x_slab, wg, wu, wd, j)
    return combine(pos, y_slab, x2d, routed, g, b)


def kernel(x, positions, w_in, conv_dw, conv_gn_g, conv_gn_b, q_norm_g, w_q_b, kv_norm_g, w_kv_b, w_o, ln_mix_g, ln_mix_b, ffn_w_gate, ffn_w_up, ffn_w_down, router_w, exp_w_gate, exp_w_up, exp_w_down, ln_ffn_g, ln_ffn_b):
    bsz, seq, d = x.shape
    assert d == SLAB * LANES
    t = bsz * seq
    depth = w_in.shape[0]
    d_conv = conv_dw.shape[2]

    win = jnp.pad(w_in, ((0, 0), (0, 0), (0, LANES - QK_ROPE_DIM))).astype(BF16)
    wq = w_q_b.reshape(depth, Q_LORA_RANK, N_HEADS, QK_NOPE_DIM + QK_ROPE_DIM)
    wq = jnp.pad(wq, ((0, 0), (0, 0), (0, 0), (0, HEAD_PAD - QK_NOPE_DIM - QK_ROPE_DIM)))
    wq = wq.reshape(depth, Q_LORA_RANK, N_HEADS * HEAD_PAD).astype(BF16)
    wkv = w_kv_b.reshape(depth, KV_LORA_RANK, N_HEADS, QK_NOPE_DIM + V_HEAD_DIM)
    wk = wkv[..., :QK_NOPE_DIM].reshape(depth, KV_LORA_RANK, N_HEADS * QK_NOPE_DIM).astype(BF16)
    wv = wkv[..., QK_NOPE_DIM:].reshape(depth, KV_LORA_RANK, N_HEADS * V_HEAD_DIM).astype(BF16)
    wo = w_o.astype(BF16)
    wr = jnp.pad(router_w, ((0, 0), (0, 0), (0, LANES - N_EXPERTS)))

    rope_c, rope_sn, rope_sp = rope_tables(positions.reshape(t, 1))

    x2d = x.reshape(t, d)
    for l in range(depth):
        u_conv, q, k, v = in_proj(x2d, win[l], q_norm_g[l][None], wq[l], kv_norm_g[l][None], wk[l], wv[l],
                                  rope_c, rope_sn, rope_sp)
        y_conv = conv_module(u_conv.reshape(bsz, seq, 2 * d_conv), conv_dw[l], conv_gn_g[l][None], conv_gn_b[l][None])
        y_att = attention(q.reshape(bsz, seq, -1), k.reshape(bsz, seq, -1), v.reshape(bsz, seq, -1))
        is_moe = l % 2 == 1
        mixed = out_proj(y_conv.reshape(t, d_conv), y_att.reshape(t, -1), wo[l, :d_conv], wo[l, d_conv:], x2d,
                         ln_mix_g[l][None], ln_mix_b[l][None], with_slab=is_moe)
        j = l // 2
        if is_moe:
            x2d = moe_ffn(mixed[0], mixed[1], wr[j], exp_w_gate, exp_w_up, exp_w_down, j,
                          ln_ffn_g[l][None], ln_ffn_b[l][None])
        else:
            x2d = dense_ffn(mixed[0], ffn_w_gate, ffn_w_up, ffn_w_down, j, ln_ffn_g[l][None], ln_ffn_b[l][None])
    return x2d.reshape(bsz, seq, d)
```

```python
import functools
import math

import jax
import jax.numpy as jnp
from jax import lax
from jax.experimental import pallas as pl
from jax.experimental.pallas import tpu as pltpu

F32 = jnp.float32
BF16 = jnp.bfloat16

CONV_GROUPS = 8
CONV_WIDTH = 31
N_HEADS = 8
QK_NOPE_DIM = 128
QK_ROPE_DIM = 64
V_HEAD_DIM = 128
Q_LORA_RANK = 512
KV_LORA_RANK = 256
ROPE_THETA = 10000.0
N_EXPERTS = 8
DEPTH = 4
ALPHA = (2.0 * DEPTH) ** 0.25
LN_EPS = 1e-5
RMS_EPS = 1e-6

LANES = 128
SUBLANES = 8
VMEM_LIMIT = 56 * 1024 * 1024

HEAD_PAD = 2 * LANES
VT_ROWS = 128 + 16
CONV_HALO = 32

TM_PROJ = 512
TS_CONV = 256
TQ_ATT = 512
HEADS_PER_STEP = 4
TM_OUT = 512
TM_FFN = 1024
TM_HALF = TM_FFN // 2
TF_FFN = 256
TM_ROUTE = 512
TM_COMB = 256
INV_UNROLL = 8
SLAB = 2048 // LANES


def _cparams(sem):
    return pltpu.CompilerParams(dimension_semantics=sem, vmem_limit_bytes=VMEM_LIMIT)


def _const_spec(shape):
    nd = len(shape)
    return pl.BlockSpec(shape, lambda *_: (0,) * nd, pipeline_mode=pl.Buffered(1))


def _layer_norm(z, g, b):
    mu = jnp.mean(z, axis=-1, keepdims=True)
    d = z - mu
    var = jnp.mean(d * d, axis=-1, keepdims=True)
    return d * lax.rsqrt(var + LN_EPS) * g + b


def _rope(x, c, sn, sp):
    return x * c + pltpu.roll(x, LANES - QK_ROPE_DIM // 2, 1) * sn + pltpu.roll(x, QK_ROPE_DIM // 2, 1) * sp


def _rope_table_body(pos_ref, invf_ref, c_ref, sn_ref, sp_ref):
    ang = pos_ref[...].astype(F32) * invf_ref[...]
    c = jnp.cos(ang)
    s = jnp.sin(ang)
    lane = lax.broadcasted_iota(jnp.int32, ang.shape, 1)
    half = QK_ROPE_DIM // 2
    c_ref[...] = jnp.where(lane < QK_ROPE_DIM, c, 0.0)
    sn_ref[...] = jnp.where(lane < half, -s, 0.0)
    sp_ref[...] = jnp.where((lane >= half) & (lane < QK_ROPE_DIM), s, 0.0)


def rope_tables(pos_col):
    t = pos_col.shape[0]
    tm = 1024
    half = QK_ROPE_DIM // 2
    inv_freq = ROPE_THETA ** (-jnp.arange(0, QK_ROPE_DIM, 2, dtype=F32) / QK_ROPE_DIM)
    invf = jnp.concatenate([inv_freq, inv_freq, jnp.zeros((LANES - 2 * half,), F32)]).reshape(1, LANES)
    out = jax.ShapeDtypeStruct((t, LANES), F32)
    row = pl.BlockSpec((tm, LANES), lambda i: (i, 0))
    return pl.pallas_call(
        _rope_table_body,
        grid=(t // tm,),
        in_specs=[pl.BlockSpec((tm, 1), lambda i: (i, 0)), pl.BlockSpec((1, LANES), lambda i: (0, 0))],
        out_specs=[row, row, row],
        out_shape=[out, out, out],
        compiler_params=_cparams(("arbitrary",)),
        name="rope_tables",
    )(pos_col, invf)


def _in_proj_body(x_ref, win_ref, qg_ref, wq_ref, kvg_ref, wk_ref, wv_ref, c_ref, sn_ref, sp_ref,
                  uconv_ref, qt_ref, k_ref, vt_ref, *, d_conv2, scale):
    xb = x_ref[...].astype(BF16)
    c, sn, sp = c_ref[...], sn_ref[...], sp_ref[...]
    o_q = d_conv2
    o_kv = o_q + Q_LORA_RANK
    o_kr = o_kv + KV_LORA_RANK

    uconv_ref[...] = jnp.dot(xb, win_ref[:, :d_conv2], preferred_element_type=F32)

    uq = jnp.dot(xb, win_ref[:, o_q:o_kv], preferred_element_type=F32)
    ms = jnp.mean(uq * uq, axis=-1, keepdims=True)
    q_lat = (uq * lax.rsqrt(ms + RMS_EPS) * qg_ref[...]).astype(BF16)
    q = jnp.dot(q_lat, wq_ref[...], preferred_element_type=F32) * scale
    for h in range(N_HEADS):
        lo = h * HEAD_PAD
        qt_ref[0, lo:lo + LANES, :] = q[:, lo:lo + LANES].T.astype(BF16)
        qt_ref[0, lo + LANES:lo + HEAD_PAD, :] = _rope(q[:, lo + LANES:lo + HEAD_PAD], c, sn, sp).T.astype(BF16)

    ukv = jnp.dot(xb, win_ref[:, o_kv:o_kr], preferred_element_type=F32)
    ms = jnp.mean(ukv * ukv, axis=-1, keepdims=True)
    kv_lat = (ukv * lax.rsqrt(ms + RMS_EPS) * kvg_ref[...]).astype(BF16)
    v = jnp.dot(kv_lat, wv_ref[...], preferred_element_type=F32)
    ones = jnp.ones((VT_ROWS - V_HEAD_DIM, v.shape[0]), BF16)
    for h in range(N_HEADS):
        lo = h * VT_ROWS
        vt_ref[0, 0, lo:lo + V_HEAD_DIM, :] = v[:, h * V_HEAD_DIM:(h + 1) * V_HEAD_DIM].T.astype(BF16)
        vt_ref[0, 0, lo + V_HEAD_DIM:lo + VT_ROWS, :] = ones
    kn = jnp.dot(kv_lat, wk_ref[...], preferred_element_type=F32)
    ukr = jnp.dot(xb, win_ref[:, o_kr:o_kr + LANES], preferred_element_type=F32)
    kr = _rope(ukr, c, sn, sp).astype(BF16)
    for h in range(N_HEADS):
        lo = h * HEAD_PAD
        k_ref[:, lo:lo + LANES] = kn[:, h * LANES:(h + 1) * LANES].astype(BF16)
        k_ref[:, lo + LANES:lo + HEAD_PAD] = kr


def in_proj(x2d, seq, win, qg, wq, kvg, wk, wv, rope_c, rope_sn, rope_sp):
    t, d = x2d.shape
    tm = TM_PROJ
    assert tm == TQ_ATT and seq % tm == 0
    nb = seq // tm
    bsz = t // seq
    d_conv2 = win.shape[1] - (Q_LORA_RANK + KV_LORA_RANK + LANES)
    scale = float((QK_NOPE_DIM + QK_ROPE_DIM) ** -0.5 * math.log2(math.e))
    row = lambda w: pl.BlockSpec((tm, w), lambda i: (i, 0))
    return pl.pallas_call(
        functools.partial(_in_proj_body, d_conv2=d_conv2, scale=scale),
        grid=(t // tm,),
        in_specs=[row(d), _const_spec(win.shape), _const_spec(qg.shape), _const_spec(wq.shape),
                  _const_spec(kvg.shape), _const_spec(wk.shape), _const_spec(wv.shape),
                  row(LANES), row(LANES), row(LANES)],
        out_specs=[row(d_conv2),
                   pl.BlockSpec((1, N_HEADS * HEAD_PAD, tm), lambda i: (i // nb, 0, i % nb)),
                   row(N_HEADS * HEAD_PAD),
                   pl.BlockSpec((1, 1, N_HEADS * VT_ROWS, tm), lambda i: (i // nb, i % nb, 0, 0))],
        out_shape=[jax.ShapeDtypeStruct((t, d_conv2), F32),
                   jax.ShapeDtypeStruct((bsz, N_HEADS * HEAD_PAD, seq), BF16),
                   jax.ShapeDtypeStruct((t, N_HEADS * HEAD_PAD), BF16),
                   jax.ShapeDtypeStruct((bsz, nb, N_HEADS * VT_ROWS, tm), BF16)],
        compiler_params=_cparams(("arbitrary",)),
        name="in_proj",
    )(x2d, win, qg, wq, kvg, wk, wv, rope_c, rope_sn, rope_sp)


def _conv_body(a_ref, g_ref, w_ref, gng_ref, gnb_ref, o_ref, hext, shifted):
    ts = a_ref.shape[1]
    s = pl.program_id(1)

    @pl.when(s == 0)
    def _():
        hext[0:CONV_HALO, :] = jnp.zeros((CONV_HALO, hext.shape[1]), F32)

    @pl.when(s > 0)
    def _():
        hext[0:CONV_HALO, :] = hext[ts:ts + CONV_HALO, :]

    hext[CONV_HALO:CONV_HALO + ts, :] = a_ref[0] * jax.nn.sigmoid(g_ref[0])

    first = CONV_HALO - (CONV_WIDTH - 1)
    n_groups = hext.shape[1] // LANES
    for c in range(n_groups):
        lanes = slice(c * LANES, (c + 1) * LANES)
        acc = None
        for b in range(SUBLANES):
            offs = [o for o in range(first, first + CONV_WIDTH) if o % SUBLANES == b]
            if b:
                n = offs[-1] - b + ts
                shifted[b, 0:n, :] = hext[b:b + n, lanes]
            for o in offs:
                hb = shifted[b, o - b:o - b + ts, :] if b else hext[o:o + ts, lanes]
                term = hb * w_ref[o - first:o - first + 1, lanes]
                acc = term if acc is None else acc + term
        mu = jnp.mean(acc, axis=-1, keepdims=True)
        d = acc - mu
        var = jnp.mean(d * d, axis=-1, keepdims=True)
        y = d * lax.rsqrt(var + LN_EPS) * gng_ref[:, lanes] + gnb_ref[:, lanes]
        o_ref[0, :, lanes] = (y * jax.nn.sigmoid(y)).astype(BF16)


def conv_module(u_conv, w_dw, gn_g, gn_b):
    b, s, d2 = u_conv.shape
    dc = d2 // 2
    assert dc // CONV_GROUPS == LANES
    ts = TS_CONV
    return pl.pallas_call(
        _conv_body,
        grid=(b, s // ts),
        in_specs=[pl.BlockSpec((1, ts, dc), lambda i, j: (i, j, 0)),
                  pl.BlockSpec((1, ts, dc), lambda i, j: (i, j, 1)),
                  pl.BlockSpec((CONV_WIDTH, dc), lambda i, j: (0, 0)),
                  pl.BlockSpec((1, dc), lambda i, j: (0, 0)),
                  pl.BlockSpec((1, dc), lambda i, j: (0, 0))],
        out_specs=pl.BlockSpec((1, ts, dc), lambda i, j: (i, j, 0)),
        out_shape=jax.ShapeDtypeStruct((b, s, dc), BF16),
        scratch_shapes=[pltpu.VMEM((CONV_HALO + ts, dc), F32), pltpu.VMEM((SUBLANES, CONV_HALO + ts, LANES), F32)],
        compiler_params=_cparams(("arbitrary", "arbitrary")),
        name="conv_module",
    )(u_conv, u_conv, w_dw, gn_g, gn_b)


def _attn_body(qt_ref, k_ref, vt_ref, o_ref, m_sc, acc_sc):
    tq = qt_ref.shape[2]
    i = pl.program_id(2)
    m_sc[...] = jnp.full(m_sc.shape, -jnp.inf, F32)
    acc_sc[...] = jnp.zeros(acc_sc.shape, F32)

    def block(j, masked):
        start = pl.multiple_of(j * tq, tq)
        for h in range(HEADS_PER_STEP):
            dims = slice(h * HEAD_PAD, (h + 1) * HEAD_PAD)
            st = jnp.dot(k_ref[0, pl.ds(start, tq), dims], qt_ref[0, dims, :], preferred_element_type=F32)
            if masked:
                key = lax.broadcasted_iota(jnp.int32, st.shape, 0)
                query = lax.broadcasted_iota(jnp.int32, st.shape, 1)
                st = jnp.where(key <= query, st, -jnp.inf)
            m_prev = m_sc[h]
            m_new = jnp.maximum(m_prev, jnp.max(st, axis=0, keepdims=True))
            a = jnp.exp2(m_prev - m_new)
            pt = jnp.exp2(st - m_new).astype(BF16)
            vt = vt_ref[0, j, h * VT_ROWS:(h + 1) * VT_ROWS, :]
            acc_sc[h] = a * acc_sc[h] + jnp.dot(vt, pt, preferred_element_type=F32)
            m_sc[h] = m_new

    def body(j, carry):
        block(j, False)
        return carry

    lax.fori_loop(0, i, body, 0)
    block(i, True)
    for h in range(HEADS_PER_STEP):
        acc = acc_sc[h]
        out_t = acc[:V_HEAD_DIM, :] / acc[V_HEAD_DIM:V_HEAD_DIM + 1, :]
        o_ref[0, :, h * V_HEAD_DIM:(h + 1) * V_HEAD_DIM] = out_t.T.astype(BF16)


def attention(qt, k, vt):
    b, s, _ = k.shape
    tq = TQ_ATT
    hp = HEADS_PER_STEP
    return pl.pallas_call(
        _attn_body,
        grid=(b, N_HEADS // hp, s // tq),
        in_specs=[pl.BlockSpec((1, hp * HEAD_PAD, tq), lambda bi, h, i: (bi, h, i)),
                  pl.BlockSpec((1, s, hp * HEAD_PAD), lambda bi, h, i: (bi, 0, h)),
                  pl.BlockSpec((1, s // tq, hp * VT_ROWS, tq), lambda bi, h, i: (bi, 0, h, 0))],
        out_specs=pl.BlockSpec((1, tq, hp * V_HEAD_DIM), lambda bi, h, i: (bi, i, h)),
        out_shape=jax.ShapeDtypeStruct((b, s, N_HEADS * V_HEAD_DIM), BF16),
        scratch_shapes=[pltpu.VMEM((hp, 1, tq), F32), pltpu.VMEM((hp, VT_ROWS, tq), F32)],
        compiler_params=_cparams(("arbitrary", "arbitrary", "arbitrary")),
        name="attention",
    )(qt, k, vt)


def _out_proj_body(yc_ref, ya_ref, wc_ref, wa_ref, x_ref, g_ref, b_ref, o_ref, *maybe_slab_ref):
    y = jnp.dot(yc_ref[...], wc_ref[...], preferred_element_type=F32)
    y = y + jnp.dot(ya_ref[...], wa_ref[...], preferred_element_type=F32)
    o_ref[...] = _layer_norm(ALPHA * x_ref[...] + y, g_ref[...], b_ref[...])
    for slab_ref in maybe_slab_ref:
        tm = o_ref.shape[0]
        for c in range(SLAB):
            slab_ref[pl.ds(c, tm, stride=SLAB), :] = o_ref[:, c * LANES:(c + 1) * LANES]


def out_proj(y_conv, y_att, wo_conv, wo_att, x2d, g, b, with_slab):
    t, d = x2d.shape
    tm = TM_OUT
    row = lambda w: pl.BlockSpec((tm, w), lambda i: (i, 0))
    out_specs = [row(d)]
    out_shape = [jax.ShapeDtypeStruct((t, d), F32)]
    if with_slab:
        out_specs.append(pl.BlockSpec((tm * SLAB, LANES), lambda i: (i, 0)))
        out_shape.append(jax.ShapeDtypeStruct((t * SLAB, LANES), F32))
    return pl.pallas_call(
        _out_proj_body,
        grid=(t // tm,),
        in_specs=[row(y_conv.shape[1]), row(y_att.shape[1]), _const_spec(wo_conv.shape), _const_spec(wo_att.shape),
                  row(d), _const_spec(g.shape), _const_spec(b.shape)],
        out_specs=out_specs,
        out_shape=out_shape,
        compiler_params=_cparams(("arbitrary",)),
        name="out_proj",
    )(y_conv, y_att, wo_conv, wo_att, x2d, g, b)


def _swiglu_chunk(xb, wg, wu, wd):
    g = jnp.dot(xb, wg, preferred_element_type=F32)
    u = jnp.dot(xb, wu, preferred_element_type=F32)
    h = (g * jax.nn.sigmoid(g) * u).astype(BF16)
    return jnp.dot(h, wd, preferred_element_type=F32)


def _half_rows(hf):
    return slice(hf * TM_HALF, (hf + 1) * TM_HALF)


def _ffn_body(x_ref, wg_ref, wu_ref, wd_ref, g_ref, b_ref, o_ref, xb_sc):
    f = pl.program_id(1)

    @pl.when(f == 0)
    def _():
        xb_sc[...] = x_ref[...].astype(BF16)
        o_ref[...] = jnp.zeros(o_ref.shape, F32)

    wg, wu, wd = wg_ref[...].astype(BF16), wu_ref[...].astype(BF16), wd_ref[...].astype(BF16)
    for hf in range(TM_FFN // TM_HALF):
        rows = _half_rows(hf)
        o_ref[rows, :] += _swiglu_chunk(xb_sc[rows, :], wg, wu, wd)

    @pl.when(f == pl.num_programs(1) - 1)
    def _():
        for hf in range(TM_FFN // TM_HALF):
            rows = _half_rows(hf)
            o_ref[rows, :] = _layer_norm(ALPHA * x_ref[rows, :] + o_ref[rows, :], g_ref[...], b_ref[...])


def dense_ffn(x2d, wg, wu, wd, j, g, b):
    t, d = x2d.shape
    dff = wg.shape[2]
    tm, tf = TM_FFN, TF_FFN
    return pl.pallas_call(
        _ffn_body,
        grid=(t // tm, dff // tf),
        in_specs=[pl.BlockSpec((tm, d), lambda i, f: (i, 0), pipeline_mode=pl.Buffered(1)),
                  pl.BlockSpec((None, d, tf), lambda i, f: (j, 0, f)),
                  pl.BlockSpec((None, d, tf), lambda i, f: (j, 0, f)),
                  pl.BlockSpec((None, tf, d), lambda i, f: (j, f, 0)),
                  pl.BlockSpec((1, d), lambda i, f: (0, 0)),
                  pl.BlockSpec((1, d), lambda i, f: (0, 0))],
        out_specs=pl.BlockSpec((tm, d), lambda i, f: (i, 0)),
        out_shape=jax.ShapeDtypeStruct((t, d), F32),
        scratch_shapes=[pltpu.VMEM((tm, d), BF16)],
        compiler_params=_cparams(("arbitrary", "arbitrary")),
        name="dense_ffn",
    )(x2d, wg, wu, wd, g, b)


_L_IDX, _L_RANK, _L_GATE = 0, 2, 4


def _router_body(x_ref, wr_ref, o_ref, cnt_ref, base_sc):
    tm = x_ref.shape[0]

    @pl.when(pl.program_id(0) == 0)
    def _():
        base_sc[...] = jnp.zeros(base_sc.shape, F32)

    x = x_ref[...]
    w = wr_ref[...]
    xh = x.astype(BF16)
    xl = (x - xh.astype(F32)).astype(BF16)
    wh = w.astype(BF16)
    wl = (w - wh.astype(F32)).astype(BF16)
    logits = (jnp.dot(xh, wh, preferred_element_type=F32) + jnp.dot(xl, wh, preferred_element_type=F32)
              + jnp.dot(xh, wl, preferred_element_type=F32))
    lane = lax.broadcasted_iota(jnp.int32, logits.shape, 1).astype(F32)
    lg = jnp.where(lane < N_EXPERTS, logits, -jnp.inf)
    m1 = jnp.max(lg, axis=-1, keepdims=True)
    i1 = jnp.min(jnp.where(lg == m1, lane, float(LANES)), axis=-1, keepdims=True)
    oh1 = lane == i1
    lg2 = jnp.where(oh1, -jnp.inf, lg)
    m2 = jnp.max(lg2, axis=-1, keepdims=True)
    i2 = jnp.min(jnp.where(lg2 == m2, lane, float(LANES)), axis=-1, keepdims=True)
    oh2 = lane == i2
    e2 = jnp.exp(m2 - m1)
    g1 = 1.0 / (1.0 + e2)
    g2 = e2 / (1.0 + e2)
    cnt = jnp.where(oh1 | oh2, 1.0, 0.0)
    r = lax.broadcasted_iota(jnp.int32, (tm, tm), 0)
    c = lax.broadcasted_iota(jnp.int32, (tm, tm), 1)
    strict_lower = jnp.where(c < r, 1.0, 0.0).astype(BF16)
    before = jnp.dot(strict_lower, cnt.astype(BF16), preferred_element_type=F32) + base_sc[...]
    r1 = jnp.sum(jnp.where(oh1, before, 0.0), axis=-1, keepdims=True)
    r2 = jnp.sum(jnp.where(oh2, before, 0.0), axis=-1, keepdims=True)
    base_sc[...] += jnp.sum(cnt, axis=0, keepdims=True)
    cnt_ref[...] = base_sc[...]
    packed = jnp.zeros(logits.shape, F32)
    for ln, val in ((_L_IDX, i1), (_L_IDX + 1, i2), (_L_RANK, r1), (_L_RANK + 1, r2), (_L_GATE, g1), (_L_GATE + 1, g2)):
        packed = jnp.where(lane == float(ln), val, packed)
    o_ref[...] = packed


def router(x2d, w_router_pad):
    t, d = x2d.shape
    tm = TM_ROUTE
    return pl.pallas_call(
        _router_body,
        grid=(t // tm,),
        in_specs=[pl.BlockSpec((tm, d), lambda i: (i, 0)), pl.BlockSpec((d, LANES), lambda i: (0, 0))],
        out_specs=[pl.BlockSpec((tm, LANES), lambda i: (i, 0)), pl.BlockSpec((1, LANES), lambda i: (0, 0))],
        out_shape=[jax.ShapeDtypeStruct((t, LANES), F32), jax.ShapeDtypeStruct((1, LANES), F32)],
        scratch_shapes=[pltpu.VMEM((1, LANES), F32)],
        compiler_params=_cparams(("arbitrary",)),
        name="router",
    )(x2d, w_router_pad)


def _invert_body(pos_ref, o_ref):
    def zero(r8, carry):
        for u in range(INV_UNROLL):
            o_ref[r8 * INV_UNROLL + u] = 0
        return carry

    lax.fori_loop(0, o_ref.shape[0] // INV_UNROLL, zero, 0)

    def put(a8, carry):
        for u in range(INV_UNROLL):
            a = a8 * INV_UNROLL + u
            o_ref[pos_ref[a]] = lax.shift_right_logical(a, 1)
        return carry

    lax.fori_loop(0, pos_ref.shape[0] // INV_UNROLL, put, 0)


def invert_positions(pos_flat, n_rows):
    return pl.pallas_call(
        _invert_body,
        in_specs=[pl.BlockSpec(memory_space=pltpu.SMEM)],
        out_specs=pl.BlockSpec(memory_space=pltpu.SMEM),
        out_shape=jax.ShapeDtypeStruct((n_rows,), jnp.int32),
        name="invert_positions",
    )(pos_flat)


def _slab_rows(row):
    start = row * SLAB
    return pl.ds(start if isinstance(start, int) else pl.multiple_of(start, SLAB), SLAB)


def _slab_copy(src_hbm, src_row, dst, dst_row, sem):
    return pltpu.make_async_copy(src_hbm.at[_slab_rows(src_row), :], dst.at[_slab_rows(dst_row), :], sem)


def _experts_body(te_ref, nv_ref, tr_ref, rt_ref, x_hbm, wg_ref, wu_ref, wd_ref, o_ref,
                  gbuf, xb_sc, acc_sc, sem, *, per_step):
    tm = TM_FFN
    i = pl.program_id(0)
    f = pl.program_id(1)
    nf = pl.num_programs(1)
    valid = i < nv_ref[0]
    last_row = rt_ref.shape[0] - 1

    def start_gather(tile, slot):
        token = rt_ref[jnp.minimum(tile * tm + slot, last_row)]
        _slab_copy(x_hbm, token, gbuf, slot, sem).start()

    def wait_gathers():
        pltpu.make_async_copy(x_hbm.at[pl.ds(0, gbuf.shape[0]), :], gbuf, sem).wait()

    @pl.when(valid & (f == 0))
    def _():
        @pl.when(i == 0)
        def _():
            def first_tile(slot, carry):
                start_gather(0, slot)
                return carry

            lax.fori_loop(0, gbuf.shape[0] // SLAB, first_tile, 0)

        wait_gathers()
        for c in range(SLAB):
            xb_sc[:, c * LANES:(c + 1) * LANES] = gbuf[pl.ds(c, tm, stride=SLAB), :].astype(BF16)
        acc_sc[...] = jnp.zeros(acc_sc.shape, F32)

    @pl.when(jnp.logical_not(valid) & (f == 0))
    def _():
        o_ref[...] = jnp.zeros(o_ref.shape, F32)

    @pl.when(valid)
    def _():
        nxt = jnp.minimum(i + 1, pl.num_programs(0) - 1)
        for u in range(per_step):
            start_gather(nxt, f * per_step + u)
        wg, wu, wd = wg_ref[...].astype(BF16), wu_ref[...].astype(BF16), wd_ref[...].astype(BF16)
        rows = _half_rows(0)
        acc_sc[rows, :] += _swiglu_chunk(xb_sc[rows, :], wg, wu, wd)

        @pl.when(tr_ref[i] > TM_HALF)
        def _():
            rows = _half_rows(1)
            acc_sc[rows, :] += _swiglu_chunk(xb_sc[rows, :], wg, wu, wd)

    @pl.when(valid & (f == nf - 1))
    def _():
        for c in range(SLAB):
            o_ref[pl.ds(c, tm, stride=SLAB), :] = acc_sc[:, c * LANES:(c + 1) * LANES]

        @pl.when(i == nv_ref[0] - 1)
        def _():
            wait_gathers()


def grouped_experts(tile_expert, n_valid, tile_rows, row_token, x_slab, wg, wu, wd, j):
    n_tiles = tile_expert.shape[0]
    d, dff = wg.shape[2], wg.shape[3]
    tm, tf = TM_FFN, TF_FFN
    nf = dff // tf
    per_step = -(-tm // nf)

    def f_idx(i, f, nv):
        return jnp.where(i < nv[0], f, nf - 1)

    return pl.pallas_call(
        functools.partial(_experts_body, per_step=per_step),
        grid_spec=pltpu.PrefetchScalarGridSpec(
            num_scalar_prefetch=4,
            grid=(n_tiles, nf),
            in_specs=[pl.BlockSpec(memory_space=pl.ANY),
                      pl.BlockSpec((None, None, d, tf), lambda i, f, te, nv, tr, rt: (j, te[i], 0, f_idx(i, f, nv))),
                      pl.BlockSpec((None, None, d, tf), lambda i, f, te, nv, tr, rt: (j, te[i], 0, f_idx(i, f, nv))),
                      pl.BlockSpec((None, None, tf, d), lambda i, f, te, nv, tr, rt: (j, te[i], f_idx(i, f, nv), 0))],
            out_specs=pl.BlockSpec((tm * SLAB, LANES), lambda i, f, te, nv, tr, rt: (i, 0),
                                   pipeline_mode=pl.Buffered(1)),
            scratch_shapes=[pltpu.VMEM((nf * per_step * SLAB, LANES), F32), pltpu.VMEM((tm, d), BF16),
                            pltpu.VMEM((tm, d), F32), pltpu.SemaphoreType.DMA],
        ),
        out_shape=jax.ShapeDtypeStruct((n_tiles * tm * SLAB, LANES), F32),
        compiler_params=_cparams(("arbitrary", "arbitrary")),
        name="grouped_experts",
    )(tile_expert, n_valid, tile_rows, row_token, x_slab, wg, wu, wd)


def _combine_body(pos_ref, y_hbm, x_ref, rt_ref, g_ref, b_ref, o_ref, gbuf_a, gbuf_b, z_sc, sem_a, sem_b):
    tm = TM_COMB
    i = pl.program_id(0)

    def start_gathers(sub_tile, gbuf, sem):
        def group(gi, carry):
            for u in range(INV_UNROLL):
                r = gi * INV_UNROLL + u
                t = sub_tile * tm + r
                _slab_copy(y_hbm, pos_ref[2 * t], gbuf, r, sem).start()
                _slab_copy(y_hbm, pos_ref[2 * t + 1], gbuf, tm + r, sem).start()
            return carry

        lax.fori_loop(0, tm // INV_UNROLL, group, 0)

    def finish(half, gbuf, sem):
        pltpu.make_async_copy(y_hbm.at[pl.ds(0, gbuf.shape[0]), :], gbuf, sem).wait()
        rows = slice(half * tm, (half + 1) * tm)
        g1 = rt_ref[rows, _L_GATE:_L_GATE + 1]
        g2 = rt_ref[rows, _L_GATE + 1:_L_GATE + 2]
        for c in range(SLAB):
            lanes = slice(c * LANES, (c + 1) * LANES)
            y1 = gbuf[pl.ds(c, tm, stride=SLAB), :]
            y2 = gbuf[pl.ds(tm * SLAB + c, tm, stride=SLAB), :]
            z_sc[:, lanes] = ALPHA * x_ref[rows, lanes] + (g1 * y1 + g2 * y2)
        o_ref[rows, :] = _layer_norm(z_sc[...], g_ref[...], b_ref[...])

    @pl.when(i == 0)
    def _():
        start_gathers(0, gbuf_a, sem_a)

    start_gathers(2 * i + 1, gbuf_b, sem_b)
    finish(0, gbuf_a, sem_a)

    @pl.when(i + 1 < pl.num_programs(0))
    def _():
        start_gathers(2 * i + 2, gbuf_a, sem_a)

    finish(1, gbuf_b, sem_b)


def combine(pos_flat, y_slab, x2d, routed, g, b):
    t, d = x2d.shape
    tm = 2 * TM_COMB
    gbuf = pltpu.VMEM((2 * TM_COMB * SLAB, LANES), F32)
    return pl.pallas_call(
        _combine_body,
        grid_spec=pltpu.PrefetchScalarGridSpec(
            num_scalar_prefetch=1,
            grid=(t // tm,),
            in_specs=[pl.BlockSpec(memory_space=pl.ANY),
                      pl.BlockSpec((tm, d), lambda i, pos: (i, 0)),
                      pl.BlockSpec((tm, LANES), lambda i, pos: (i, 0)),
                      pl.BlockSpec((1, d), lambda i, pos: (0, 0)),
                      pl.BlockSpec((1, d), lambda i, pos: (0, 0))],
            out_specs=pl.BlockSpec((tm, d), lambda i, pos: (i, 0)),
            scratch_shapes=[gbuf, gbuf, pltpu.VMEM((TM_COMB, d), F32),
                            pltpu.SemaphoreType.DMA, pltpu.SemaphoreType.DMA],
        ),
        out_shape=jax.ShapeDtypeStruct((t, d), F32),
        compiler_params=_cparams(("arbitrary",)),
        name="moe_combine",
    )(pos_flat, y_slab, x2d, routed, g, b)


def moe_ffn(x2d, x_slab, w_router_pad, wg, wu, wd, j, g, b):
    t, d = x2d.shape
    tm = TM_FFN
    n_tiles = -(-(t * 2) // tm) + N_EXPERTS
    routed, counts = router(x2d, w_router_pad)
    idx = routed[:, _L_IDX:_L_IDX + 2].astype(jnp.int32)
    rank = routed[:, _L_RANK:_L_RANK + 2].astype(jnp.int32)
    counts = counts[0, :N_EXPERTS].astype(jnp.int32)
    tiles_per_e = (counts + tm - 1) // tm
    tile_end = jnp.cumsum(tiles_per_e)
    tile_start = tile_end - tiles_per_e
    pos = ((tile_start * tm)[idx] + rank).reshape(-1)
    n_valid = tile_end[-1:]
    tile_id = jnp.minimum(jnp.arange(n_tiles, dtype=jnp.int32), n_valid[0] - 1)
    tile_expert = jnp.sum(tile_id[:, None] >= tile_end[None, :], axis=1).astype(jnp.int32)
    tile_rows = jnp.minimum(counts[tile_expert] - (tile_id - tile_start[tile_expert]) * tm, tm).astype(jnp.int32)
    row_token = invert_positions(pos, n_tiles * tm)
    y_slab = grouped_experts(tile_expert, n_valid.astype(jnp.int32), tile_rows, row_token, x_slab, wg, wu, wd, j)
    return combine(pos, y_slab, x2d, routed, g, b)


def mixer_weights(w_in, w_q_b, w_kv_b):
    depth = w_in.shape[0]
    win = jnp.pad(w_in, ((0, 0), (0, 0), (0, LANES - QK_ROPE_DIM))).astype(BF16)
    wq = w_q_b.reshape(depth, Q_LORA_RANK, N_HEADS, QK_NOPE_DIM + QK_ROPE_DIM)
    wq = jnp.pad(wq, ((0, 0), (0, 0), (0, 0), (0, HEAD_PAD - QK_NOPE_DIM - QK_ROPE_DIM)))
    wq = wq.reshape(depth, Q_LORA_RANK, N_HEADS * HEAD_PAD).astype(BF16)
    wkv = w_kv_b.reshape(depth, KV_LORA_RANK, N_HEADS, QK_NOPE_DIM + V_HEAD_DIM)
    wk = wkv[..., :QK_NOPE_DIM].reshape(depth, KV_LORA_RANK, N_HEADS * QK_NOPE_DIM).astype(BF16)
    wv = wkv[..., QK_NOPE_DIM:].reshape(depth, KV_LORA_RANK, N_HEADS * V_HEAD_DIM).astype(BF16)
    return win, wq, wk, wv


def kernel(x, positions, w_in, conv_dw, conv_gn_g, conv_gn_b, q_norm_g, w_q_b, kv_norm_g, w_kv_b, w_o, ln_mix_g, ln_mix_b, ffn_w_gate, ffn_w_up, ffn_w_down, router_w, exp_w_gate, exp_w_up, exp_w_down, ln_ffn_g, ln_ffn_b):
    bsz, seq, d = x.shape
    assert d == SLAB * LANES
    t = bsz * seq
    depth = w_in.shape[0]
    d_conv = conv_dw.shape[2]

    win, wq, wk, wv = mixer_weights(w_in, w_q_b, w_kv_b)
    wo = w_o.astype(BF16)
    wr = jnp.pad(router_w, ((0, 0), (0, 0), (0, LANES - N_EXPERTS)))

    rope_c, rope_sn, rope_sp = rope_tables(positions.reshape(t, 1))

    x2d = x.reshape(t, d)
    for l in range(depth):
        u_conv, qt, k, vt = in_proj(x2d, seq, win[l], q_norm_g[l][None], wq[l], kv_norm_g[l][None], wk[l], wv[l],
                                    rope_c, rope_sn, rope_sp)
        y_conv = conv_module(u_conv.reshape(bsz, seq, 2 * d_conv), conv_dw[l], conv_gn_g[l][None], conv_gn_b[l][None])
        y_att = attention(qt, k.reshape(bsz, seq, -1), vt)
        is_moe = l % 2 == 1
        mixed = out_proj(y_conv.reshape(t, d_conv), y_att.reshape(t, -1), wo[l, :d_conv], wo[l, d_conv:], x2d,
                         ln_mix_g[l][None], ln_mix_b[l][None], with_slab=is_moe)
        j = l // 2
        if is_moe:
            x2d = moe_ffn(mixed[0], mixed[1], wr[j], exp_w_gate, exp_w_up, exp_w_down, j,
                          ln_ffn_g[l][None], ln_ffn_b[l][None])
        else:
            x2d = dense_ffn(mixed[0], ffn_w_gate, ffn_w_up, ffn_w_down, j, ln_ffn_g[l][None], ln_ffn_b[l][None])
    return x2d.reshape(bsz, seq, d)
```

```python
import functools
import math

import jax
import jax.numpy as jnp
from jax import lax
from jax.experimental import pallas as pl
from jax.experimental.pallas import tpu as pltpu

F32 = jnp.float32
BF16 = jnp.bfloat16

CONV_GROUPS = 8
CONV_WIDTH = 31
N_HEADS = 8
QK_NOPE_DIM = 128
QK_ROPE_DIM = 64
V_HEAD_DIM = 128
Q_LORA_RANK = 512
KV_LORA_RANK = 256
ROPE_THETA = 10000.0
N_EXPERTS = 8
DEPTH = 4
ALPHA = (2.0 * DEPTH) ** 0.25
LN_EPS = 1e-5
RMS_EPS = 1e-6

LANES = 128
SUBLANES = 8
VMEM_LIMIT = 56 * 1024 * 1024

HEAD_PAD = 2 * LANES
VT_ROWS = 128 + 16
CONV_HALO = 32

TM_PROJ = 512
TS_CONV = 256
TQ_ATT = 512
HEADS_PER_STEP = 4
TM_OUT = 512
TM_FFN = 1024
TM_HALF = TM_FFN // 2
TF_FFN = 256
TM_ROUTE = 512
TM_COMB = 256
INV_UNROLL = 8
SLAB = 2048 // LANES
GATHER_PITCH = SLAB + SUBLANES


def _cparams(sem):
    return pltpu.CompilerParams(dimension_semantics=sem, vmem_limit_bytes=VMEM_LIMIT)


def _const_spec(shape):
    nd = len(shape)
    return pl.BlockSpec(shape, lambda *_: (0,) * nd, pipeline_mode=pl.Buffered(1))


def _layer_norm(z, g, b):
    mu = jnp.mean(z, axis=-1, keepdims=True)
    d = z - mu
    var = jnp.mean(d * d, axis=-1, keepdims=True)
    return d * lax.rsqrt(var + LN_EPS) * g + b


def _rope(x, c, sn, sp):
    return x * c + pltpu.roll(x, LANES - QK_ROPE_DIM // 2, 1) * sn + pltpu.roll(x, QK_ROPE_DIM // 2, 1) * sp


def _rope_table_body(pos_ref, invf_ref, c_ref, sn_ref, sp_ref):
    ang = pos_ref[...].astype(F32) * invf_ref[...]
    c = jnp.cos(ang)
    s = jnp.sin(ang)
    lane = lax.broadcasted_iota(jnp.int32, ang.shape, 1)
    half = QK_ROPE_DIM // 2
    c_ref[...] = jnp.where(lane < QK_ROPE_DIM, c, 0.0)
    sn_ref[...] = jnp.where(lane < half, -s, 0.0)
    sp_ref[...] = jnp.where((lane >= half) & (lane < QK_ROPE_DIM), s, 0.0)


def rope_tables(pos_col):
    t = pos_col.shape[0]
    tm = 1024
    half = QK_ROPE_DIM // 2
    inv_freq = ROPE_THETA ** (-jnp.arange(0, QK_ROPE_DIM, 2, dtype=F32) / QK_ROPE_DIM)
    invf = jnp.concatenate([inv_freq, inv_freq, jnp.zeros((LANES - 2 * half,), F32)]).reshape(1, LANES)
    out = jax.ShapeDtypeStruct((t, LANES), F32)
    row = pl.BlockSpec((tm, LANES), lambda i: (i, 0))
    return pl.pallas_call(
        _rope_table_body,
        grid=(t // tm,),
        in_specs=[pl.BlockSpec((tm, 1), lambda i: (i, 0)), pl.BlockSpec((1, LANES), lambda i: (0, 0))],
        out_specs=[row, row, row],
        out_shape=[out, out, out],
        compiler_params=_cparams(("arbitrary",)),
        name="rope_tables",
    )(pos_col, invf)


def _in_proj_body(x_ref, win_ref, qg_ref, wq_ref, kvg_ref, wk_ref, wv_ref, c_ref, sn_ref, sp_ref,
                  uconv_ref, qt_ref, k_ref, vt_ref, *, d_conv2, scale):
    xb = x_ref[...].astype(BF16)
    c, sn, sp = c_ref[...], sn_ref[...], sp_ref[...]
    o_q = d_conv2
    o_kv = o_q + Q_LORA_RANK
    o_kr = o_kv + KV_LORA_RANK

    uconv_ref[...] = jnp.dot(xb, win_ref[:, :d_conv2], preferred_element_type=F32)

    uq = jnp.dot(xb, win_ref[:, o_q:o_kv], preferred_element_type=F32)
    ms = jnp.mean(uq * uq, axis=-1, keepdims=True)
    q_lat = (uq * lax.rsqrt(ms + RMS_EPS) * qg_ref[...]).astype(BF16)
    q = jnp.dot(q_lat, wq_ref[...], preferred_element_type=F32) * scale
    for h in range(N_HEADS):
        lo = h * HEAD_PAD
        qt_ref[0, lo:lo + LANES, :] = q[:, lo:lo + LANES].T.astype(BF16)
        qt_ref[0, lo + LANES:lo + HEAD_PAD, :] = _rope(q[:, lo + LANES:lo + HEAD_PAD], c, sn, sp).T.astype(BF16)

    ukv = jnp.dot(xb, win_ref[:, o_kv:o_kr], preferred_element_type=F32)
    ms = jnp.mean(ukv * ukv, axis=-1, keepdims=True)
    kv_lat = (ukv * lax.rsqrt(ms + RMS_EPS) * kvg_ref[...]).astype(BF16)
    v = jnp.dot(kv_lat, wv_ref[...], preferred_element_type=F32)
    ones = jnp.ones((VT_ROWS - V_HEAD_DIM, v.shape[0]), BF16)
    for h in range(N_HEADS):
        lo = h * VT_ROWS
        vt_ref[0, 0, lo:lo + V_HEAD_DIM, :] = v[:, h * V_HEAD_DIM:(h + 1) * V_HEAD_DIM].T.astype(BF16)
        vt_ref[0, 0, lo + V_HEAD_DIM:lo + VT_ROWS, :] = ones
    kn = jnp.dot(kv_lat, wk_ref[...], preferred_element_type=F32)
    ukr = jnp.dot(xb, win_ref[:, o_kr:o_kr + LANES], preferred_element_type=F32)
    kr = _rope(ukr, c, sn, sp).astype(BF16)
    for h in range(N_HEADS):
        lo = h * HEAD_PAD
        k_ref[:, lo:lo + LANES] = kn[:, h * LANES:(h + 1) * LANES].astype(BF16)
        k_ref[:, lo + LANES:lo + HEAD_PAD] = kr


def in_proj(x2d, seq, win, qg, wq, kvg, wk, wv, rope_c, rope_sn, rope_sp):
    t, d = x2d.shape
    tm = TM_PROJ
    assert tm == TQ_ATT and seq % tm == 0
    nb = seq // tm
    bsz = t // seq
    d_conv2 = win.shape[1] - (Q_LORA_RANK + KV_LORA_RANK + LANES)
    scale = float((QK_NOPE_DIM + QK_ROPE_DIM) ** -0.5 * math.log2(math.e))
    row = lambda w: pl.BlockSpec((tm, w), lambda i: (i, 0))
    return pl.pallas_call(
        functools.partial(_in_proj_body, d_conv2=d_conv2, scale=scale),
        grid=(t // tm,),
        in_specs=[row(d), _const_spec(win.shape), _const_spec(qg.shape), _const_spec(wq.shape),
                  _const_spec(kvg.shape), _const_spec(wk.shape), _const_spec(wv.shape),
                  row(LANES), row(LANES), row(LANES)],
        out_specs=[row(d_conv2),
                   pl.BlockSpec((1, N_HEADS * HEAD_PAD, tm), lambda i: (i // nb, 0, i % nb)),
                   row(N_HEADS * HEAD_PAD),
                   pl.BlockSpec((1, 1, N_HEADS * VT_ROWS, tm), lambda i: (i // nb, i % nb, 0, 0))],
        out_shape=[jax.ShapeDtypeStruct((t, d_conv2), F32),
                   jax.ShapeDtypeStruct((bsz, N_HEADS * HEAD_PAD, seq), BF16),
                   jax.ShapeDtypeStruct((t, N_HEADS * HEAD_PAD), BF16),
                   jax.ShapeDtypeStruct((bsz, nb, N_HEADS * VT_ROWS, tm), BF16)],
        compiler_params=_cparams(("arbitrary",)),
        name="in_proj",
    )(x2d, win, qg, wq, kvg, wk, wv, rope_c, rope_sn, rope_sp)


def _conv_body(a_ref, g_ref, w_ref, gng_ref, gnb_ref, o_ref, hext, shifted):
    ts = a_ref.shape[1]
    s = pl.program_id(1)

    @pl.when(s == 0)
    def _():
        hext[0:CONV_HALO, :] = jnp.zeros((CONV_HALO, hext.shape[1]), F32)

    @pl.when(s > 0)
    def _():
        hext[0:CONV_HALO, :] = hext[ts:ts + CONV_HALO, :]

    hext[CONV_HALO:CONV_HALO + ts, :] = a_ref[0] * jax.nn.sigmoid(g_ref[0])

    first = CONV_HALO - (CONV_WIDTH - 1)
    n_groups = hext.shape[1] // LANES
    for c in range(n_groups):
        lanes = slice(c * LANES, (c + 1) * LANES)
        acc = None
        for b in range(SUBLANES):
            offs = [o for o in range(first, first + CONV_WIDTH) if o % SUBLANES == b]
            if b:
                n = offs[-1] - b + ts
                shifted[b, 0:n, :] = hext[b:b + n, lanes]
            for o in offs:
                hb = shifted[b, o - b:o - b + ts, :] if b else hext[o:o + ts, lanes]
                term = hb * w_ref[o - first:o - first + 1, lanes]
                acc = term if acc is None else acc + term
        mu = jnp.mean(acc, axis=-1, keepdims=True)
        d = acc - mu
        var = jnp.mean(d * d, axis=-1, keepdims=True)
        y = d * lax.rsqrt(var + LN_EPS) * gng_ref[:, lanes] + gnb_ref[:, lanes]
        o_ref[0, :, lanes] = (y * jax.nn.sigmoid(y)).astype(BF16)


def conv_module(u_conv, w_dw, gn_g, gn_b):
    b, s, d2 = u_conv.shape
    dc = d2 // 2
    assert dc // CONV_GROUPS == LANES
    ts = TS_CONV
    return pl.pallas_call(
        _conv_body,
        grid=(b, s // ts),
        in_specs=[pl.BlockSpec((1, ts, dc), lambda i, j: (i, j, 0)),
                  pl.BlockSpec((1, ts, dc), lambda i, j: (i, j, 1)),
                  pl.BlockSpec((CONV_WIDTH, dc), lambda i, j: (0, 0)),
                  pl.BlockSpec((1, dc), lambda i, j: (0, 0)),
                  pl.BlockSpec((1, dc), lambda i, j: (0, 0))],
        out_specs=pl.BlockSpec((1, ts, dc), lambda i, j: (i, j, 0)),
        out_shape=jax.ShapeDtypeStruct((b, s, dc), BF16),
        scratch_shapes=[pltpu.VMEM((CONV_HALO + ts, dc), F32), pltpu.VMEM((SUBLANES, CONV_HALO + ts, LANES), F32)],
        compiler_params=_cparams(("arbitrary", "arbitrary")),
        name="conv_module",
    )(u_conv, u_conv, w_dw, gn_g, gn_b)


def _attn_body(qt_ref, k_ref, vt_ref, o_ref, m_sc, acc_sc):
    tq = qt_ref.shape[2]
    i = pl.program_id(2)
    m_sc[...] = jnp.full(m_sc.shape, -jnp.inf, F32)
    acc_sc[...] = jnp.zeros(acc_sc.shape, F32)

    def block(j, masked):
        start = pl.multiple_of(j * tq, tq)
        for h in range(HEADS_PER_STEP):
            dims = slice(h * HEAD_PAD, (h + 1) * HEAD_PAD)
            st = jnp.dot(k_ref[0, pl.ds(start, tq), dims], qt_ref[0, dims, :], preferred_element_type=F32)
            if masked:
                key = lax.broadcasted_iota(jnp.int32, st.shape, 0)
                query = lax.broadcasted_iota(jnp.int32, st.shape, 1)
                st = jnp.where(key <= query, st, -jnp.inf)
            m_prev = m_sc[h]
            m_new = jnp.maximum(m_prev, jnp.max(st, axis=0, keepdims=True))
            a = jnp.exp2(m_prev - m_new)
            pt = jnp.exp2(st - m_new).astype(BF16)
            vt = vt_ref[0, j, h * VT_ROWS:(h + 1) * VT_ROWS, :]
            acc_sc[h] = a * acc_sc[h] + jnp.dot(vt, pt, preferred_element_type=F32)
            m_sc[h] = m_new

    def body(j, carry):
        block(j, False)
        return carry

    lax.fori_loop(0, i, body, 0)
    block(i, True)
    for h in range(HEADS_PER_STEP):
        acc = acc_sc[h]
        out_t = acc[:V_HEAD_DIM, :] / acc[V_HEAD_DIM:V_HEAD_DIM + 1, :]
        o_ref[0, :, h * V_HEAD_DIM:(h + 1) * V_HEAD_DIM] = out_t.T.astype(BF16)


def attention(qt, k, vt):
    b, s, _ = k.shape
    tq = TQ_ATT
    hp = HEADS_PER_STEP
    return pl.pallas_call(
        _attn_body,
        grid=(b, N_HEADS // hp, s // tq),
        in_specs=[pl.BlockSpec((1, hp * HEAD_PAD, tq), lambda bi, h, i: (bi, h, i)),
                  pl.BlockSpec((1, s, hp * HEAD_PAD), lambda bi, h, i: (bi, 0, h)),
                  pl.BlockSpec((1, s // tq, hp * VT_ROWS, tq), lambda bi, h, i: (bi, 0, h, 0))],
        out_specs=pl.BlockSpec((1, tq, hp * V_HEAD_DIM), lambda bi, h, i: (bi, i, h)),
        out_shape=jax.ShapeDtypeStruct((b, s, N_HEADS * V_HEAD_DIM), BF16),
        scratch_shapes=[pltpu.VMEM((hp, 1, tq), F32), pltpu.VMEM((hp, VT_ROWS, tq), F32)],
        compiler_params=_cparams(("arbitrary", "arbitrary", "arbitrary")),
        name="attention",
    )(qt, k, vt)


def _out_proj_body(yc_ref, ya_ref, wc_ref, wa_ref, x_ref, g_ref, b_ref, o_ref, *maybe_slab_ref):
    y = jnp.dot(yc_ref[...], wc_ref[...], preferred_element_type=F32)
    y = y + jnp.dot(ya_ref[...], wa_ref[...], preferred_element_type=F32)
    o_ref[...] = _layer_norm(ALPHA * x_ref[...] + y, g_ref[...], b_ref[...])
    for slab_ref in maybe_slab_ref:
        tm = o_ref.shape[0]
        for c in range(SLAB):
            slab_ref[pl.ds(c, tm, stride=SLAB), :] = o_ref[:, c * LANES:(c + 1) * LANES]


def out_proj(y_conv, y_att, wo_conv, wo_att, x2d, g, b, with_slab):
    t, d = x2d.shape
    tm = TM_OUT
    row = lambda w: pl.BlockSpec((tm, w), lambda i: (i, 0))
    out_specs = [row(d)]
    out_shape = [jax.ShapeDtypeStruct((t, d), F32)]
    if with_slab:
        out_specs.append(pl.BlockSpec((tm * SLAB, LANES), lambda i: (i, 0)))
        out_shape.append(jax.ShapeDtypeStruct((t * SLAB, LANES), F32))
    return pl.pallas_call(
        _out_proj_body,
        grid=(t // tm,),
        in_specs=[row(y_conv.shape[1]), row(y_att.shape[1]), _const_spec(wo_conv.shape), _const_spec(wo_att.shape),
                  row(d), _const_spec(g.shape), _const_spec(b.shape)],
        out_specs=out_specs,
        out_shape=out_shape,
        compiler_params=_cparams(("arbitrary",)),
        name="out_proj",
    )(y_conv, y_att, wo_conv, wo_att, x2d, g, b)


def _swiglu_chunk(xb, wg, wu, wd):
    g = jnp.dot(xb, wg, preferred_element_type=F32)
    u = jnp.dot(xb, wu, preferred_element_type=F32)
    h = (g * jax.nn.sigmoid(g) * u).astype(BF16)
    return jnp.dot(h, wd, preferred_element_type=F32)


def _half_rows(hf):
    return slice(hf * TM_HALF, (hf + 1) * TM_HALF)


def _ffn_body(x_ref, wg_ref, wu_ref, wd_ref, g_ref, b_ref, o_ref, xb_sc):
    f = pl.program_id(1)

    @pl.when(f == 0)
    def _():
        xb_sc[...] = x_ref[...].astype(BF16)
        o_ref[...] = jnp.zeros(o_ref.shape, F32)

    wg, wu, wd = wg_ref[...].astype(BF16), wu_ref[...].astype(BF16), wd_ref[...].astype(BF16)
    for hf in range(TM_FFN // TM_HALF):
        rows = _half_rows(hf)
        o_ref[rows, :] += _swiglu_chunk(xb_sc[rows, :], wg, wu, wd)

    @pl.when(f == pl.num_programs(1) - 1)
    def _():
        for hf in range(TM_FFN // TM_HALF):
            rows = _half_rows(hf)
            o_ref[rows, :] = _layer_norm(ALPHA * x_ref[rows, :] + o_ref[rows, :], g_ref[...], b_ref[...])


def dense_ffn(x2d, wg, wu, wd, j, g, b):
    t, d = x2d.shape
    dff = wg.shape[2]
    tm, tf = TM_FFN, TF_FFN
    return pl.pallas_call(
        _ffn_body,
        grid=(t // tm, dff // tf),
        in_specs=[pl.BlockSpec((tm, d), lambda i, f: (i, 0), pipeline_mode=pl.Buffered(1)),
                  pl.BlockSpec((None, d, tf), lambda i, f: (j, 0, f)),
                  pl.BlockSpec((None, d, tf), lambda i, f: (j, 0, f)),
                  pl.BlockSpec((None, tf, d), lambda i, f: (j, f, 0)),
                  pl.BlockSpec((1, d), lambda i, f: (0, 0)),
                  pl.BlockSpec((1, d), lambda i, f: (0, 0))],
        out_specs=pl.BlockSpec((tm, d), lambda i, f: (i, 0)),
        out_shape=jax.ShapeDtypeStruct((t, d), F32),
        scratch_shapes=[pltpu.VMEM((tm, d), BF16)],
        compiler_params=_cparams(("arbitrary", "arbitrary")),
        name="dense_ffn",
    )(x2d, wg, wu, wd, g, b)


_L_IDX, _L_RANK, _L_GATE = 0, 2, 4


def _router_body(x_ref, wr_ref, o_ref, cnt_ref, base_sc):
    tm = x_ref.shape[0]

    @pl.when(pl.program_id(0) == 0)
    def _():
        base_sc[...] = jnp.zeros(base_sc.shape, F32)

    x = x_ref[...]
    w = wr_ref[...]
    xh = x.astype(BF16)
    xl = (x - xh.astype(F32)).astype(BF16)
    wh = w.astype(BF16)
    wl = (w - wh.astype(F32)).astype(BF16)
    logits = (jnp.dot(xh, wh, preferred_element_type=F32) + jnp.dot(xl, wh, preferred_element_type=F32)
              + jnp.dot(xh, wl, preferred_element_type=F32))
    lane = lax.broadcasted_iota(jnp.int32, logits.shape, 1).astype(F32)
    lg = jnp.where(lane < N_EXPERTS, logits, -jnp.inf)
    m1 = jnp.max(lg, axis=-1, keepdims=True)
    i1 = jnp.min(jnp.where(lg == m1, lane, float(LANES)), axis=-1, keepdims=True)
    oh1 = lane == i1
    lg2 = jnp.where(oh1, -jnp.inf, lg)
    m2 = jnp.max(lg2, axis=-1, keepdims=True)
    i2 = jnp.min(jnp.where(lg2 == m2, lane, float(LANES)), axis=-1, keepdims=True)
    oh2 = lane == i2
    e2 = jnp.exp(m2 - m1)
    g1 = 1.0 / (1.0 + e2)
    g2 = e2 / (1.0 + e2)
    cnt = jnp.where(oh1 | oh2, 1.0, 0.0)
    r = lax.broadcasted_iota(jnp.int32, (tm, tm), 0)
    c = lax.broadcasted_iota(jnp.int32, (tm, tm), 1)
    strict_lower = jnp.where(c < r, 1.0, 0.0).astype(BF16)
    before = jnp.dot(strict_lower, cnt.astype(BF16), preferred_element_type=F32) + base_sc[...]
    r1 = jnp.sum(jnp.where(oh1, before, 0.0), axis=-1, keepdims=True)
    r2 = jnp.sum(jnp.where(oh2, before, 0.0), axis=-1, keepdims=True)
    base_sc[...] += jnp.sum(cnt, axis=0, keepdims=True)
    cnt_ref[...] = base_sc[...]
    packed = jnp.zeros(logits.shape, F32)
    for ln, val in ((_L_IDX, i1), (_L_IDX + 1, i2), (_L_RANK, r1), (_L_RANK + 1, r2), (_L_GATE, g1), (_L_GATE + 1, g2)):
        packed = jnp.where(lane == float(ln), val, packed)
    o_ref[...] = packed


def router(x2d, w_router_pad):
    t, d = x2d.shape
    tm = TM_ROUTE
    return pl.pallas_call(
        _router_body,
        grid=(t // tm,),
        in_specs=[pl.BlockSpec((tm, d), lambda i: (i, 0)), pl.BlockSpec((d, LANES), lambda i: (0, 0))],
        out_specs=[pl.BlockSpec((tm, LANES), lambda i: (i, 0)), pl.BlockSpec((1, LANES), lambda i: (0, 0))],
        out_shape=[jax.ShapeDtypeStruct((t, LANES), F32), jax.ShapeDtypeStruct((1, LANES), F32)],
        scratch_shapes=[pltpu.VMEM((1, LANES), F32)],
        compiler_params=_cparams(("arbitrary",)),
        name="router",
    )(x2d, w_router_pad)


def _invert_body(pos_ref, o_ref):
    def zero(r8, carry):
        for u in range(INV_UNROLL):
            o_ref[r8 * INV_UNROLL + u] = 0
        return carry

    lax.fori_loop(0, o_ref.shape[0] // INV_UNROLL, zero, 0)

    def put(a8, carry):
        for u in range(INV_UNROLL):
            a = a8 * INV_UNROLL + u
            o_ref[pos_ref[a]] = lax.shift_right_logical(a, 1)
        return carry

    lax.fori_loop(0, pos_ref.shape[0] // INV_UNROLL, put, 0)


def invert_positions(pos_flat, n_rows):
    return pl.pallas_call(
        _invert_body,
        in_specs=[pl.BlockSpec(memory_space=pltpu.SMEM)],
        out_specs=pl.BlockSpec(memory_space=pltpu.SMEM),
        out_shape=jax.ShapeDtypeStruct((n_rows,), jnp.int32),
        name="invert_positions",
    )(pos_flat)


def _slab_rows(row):
    start = row * SLAB
    return pl.ds(start if isinstance(start, int) else pl.multiple_of(start, SLAB), SLAB)


def _gather_copy(src_hbm, src_row, gbuf, slot, sem):
    start = slot * GATHER_PITCH
    start = start if isinstance(start, int) else pl.multiple_of(start, SUBLANES)
    return pltpu.make_async_copy(src_hbm.at[_slab_rows(src_row), :], gbuf.at[pl.ds(start, SLAB), :], sem)


def _wait_gathers(src_hbm, gbuf, sem):
    rows = gbuf.shape[0] // GATHER_PITCH * SLAB
    pltpu.make_async_copy(src_hbm.at[pl.ds(0, rows), :], gbuf.at[pl.ds(0, rows), :], sem).wait()


def _gathered_chunk(gbuf, first_slot, n, c):
    return gbuf[pl.ds(first_slot * GATHER_PITCH + c, n, stride=GATHER_PITCH), :]


def _experts_body(te_ref, nv_ref, tr_ref, rt_ref, x_hbm, wg_ref, wu_ref, wd_ref, o_ref,
                  gbuf, xb_sc, acc_sc, sem, *, per_step):
    tm = TM_FFN
    i = pl.program_id(0)
    f = pl.program_id(1)
    nf = pl.num_programs(1)
    valid = i < nv_ref[0]
    last_row = rt_ref.shape[0] - 1

    def start_gather(tile, slot):
        token = rt_ref[jnp.minimum(tile * tm + slot, last_row)]
        _gather_copy(x_hbm, token, gbuf, slot, sem).start()

    def wait_gathers():
        _wait_gathers(x_hbm, gbuf, sem)

    @pl.when(valid & (f == 0))
    def _():
        @pl.when(i == 0)
        def _():
            def first_tile(slot, carry):
                start_gather(0, slot)
                return carry

            lax.fori_loop(0, gbuf.shape[0] // GATHER_PITCH, first_tile, 0)

        wait_gathers()
        for c in range(SLAB):
            xb_sc[:, c * LANES:(c + 1) * LANES] = _gathered_chunk(gbuf, 0, tm, c).astype(BF16)
        acc_sc[...] = jnp.zeros(acc_sc.shape, F32)

    @pl.when(jnp.logical_not(valid) & (f == 0))
    def _():
        o_ref[...] = jnp.zeros(o_ref.shape, F32)

    @pl.when(valid)
    def _():
        nxt = jnp.minimum(i + 1, pl.num_programs(0) - 1)
        for u in range(per_step):
            start_gather(nxt, f * per_step + u)
        wg, wu, wd = wg_ref[...].astype(BF16), wu_ref[...].astype(BF16), wd_ref[...].astype(BF16)
        rows = _half_rows(0)
        acc_sc[rows, :] += _swiglu_chunk(xb_sc[rows, :], wg, wu, wd)

        @pl.when(tr_ref[i] > TM_HALF)
        def _():
            rows = _half_rows(1)
            acc_sc[rows, :] += _swiglu_chunk(xb_sc[rows, :], wg, wu, wd)

    @pl.when(valid & (f == nf - 1))
    def _():
        for c in range(SLAB):
            o_ref[pl.ds(c, tm, stride=SLAB), :] = acc_sc[:, c * LANES:(c + 1) * LANES]

        @pl.when(i == nv_ref[0] - 1)
        def _():
            wait_gathers()


def grouped_experts(tile_expert, n_valid, tile_rows, row_token, x_slab, wg, wu, wd, j):
    n_tiles = tile_expert.shape[0]
    d, dff = wg.shape[2], wg.shape[3]
    tm, tf = TM_FFN, TF_FFN
    nf = dff // tf
    per_step = -(-tm // nf)

    def f_idx(i, f, nv):
        return jnp.where(i < nv[0], f, nf - 1)

    return pl.pallas_call(
        functools.partial(_experts_body, per_step=per_step),
        grid_spec=pltpu.PrefetchScalarGridSpec(
            num_scalar_prefetch=4,
            grid=(n_tiles, nf),
            in_specs=[pl.BlockSpec(memory_space=pl.ANY),
                      pl.BlockSpec((None, None, d, tf), lambda i, f, te, nv, tr, rt: (j, te[i], 0, f_idx(i, f, nv))),
                      pl.BlockSpec((None, None, d, tf), lambda i, f, te, nv, tr, rt: (j, te[i], 0, f_idx(i, f, nv))),
                      pl.BlockSpec((None, None, tf, d), lambda i, f, te, nv, tr, rt: (j, te[i], f_idx(i, f, nv), 0))],
            out_specs=pl.BlockSpec((tm * SLAB, LANES), lambda i, f, te, nv, tr, rt: (i, 0),
                                   pipeline_mode=pl.Buffered(1)),
            scratch_shapes=[pltpu.VMEM((nf * per_step * GATHER_PITCH, LANES), F32), pltpu.VMEM((tm, d), BF16),
                            pltpu.VMEM((tm, d), F32), pltpu.SemaphoreType.DMA],
        ),
        out_shape=jax.ShapeDtypeStruct((n_tiles * tm * SLAB, LANES), F32),
        compiler_params=_cparams(("arbitrary", "arbitrary")),
        name="grouped_experts",
    )(tile_expert, n_valid, tile_rows, row_token, x_slab, wg, wu, wd)


def _combine_body(pos_ref, y_hbm, x_ref, rt_ref, g_ref, b_ref, o_ref, gbuf_a, gbuf_b, z_sc, sem_a, sem_b):
    tm = TM_COMB
    i = pl.program_id(0)

    def start_gathers(sub_tile, gbuf, sem):
        def group(gi, carry):
            for u in range(INV_UNROLL):
                r = gi * INV_UNROLL + u
                t = sub_tile * tm + r
                _gather_copy(y_hbm, pos_ref[2 * t], gbuf, r, sem).start()
                _gather_copy(y_hbm, pos_ref[2 * t + 1], gbuf, tm + r, sem).start()
            return carry

        lax.fori_loop(0, tm // INV_UNROLL, group, 0)

    def finish(half, gbuf, sem):
        _wait_gathers(y_hbm, gbuf, sem)
        rows = slice(half * tm, (half + 1) * tm)
        g1 = rt_ref[rows, _L_GATE:_L_GATE + 1]
        g2 = rt_ref[rows, _L_GATE + 1:_L_GATE + 2]
        for c in range(SLAB):
            lanes = slice(c * LANES, (c + 1) * LANES)
            y1 = _gathered_chunk(gbuf, 0, tm, c)
            y2 = _gathered_chunk(gbuf, tm, tm, c)
            z_sc[:, lanes] = ALPHA * x_ref[rows, lanes] + (g1 * y1 + g2 * y2)
        o_ref[rows, :] = _layer_norm(z_sc[...], g_ref[...], b_ref[...])

    @pl.when(i == 0)
    def _():
        start_gathers(0, gbuf_a, sem_a)

    start_gathers(2 * i + 1, gbuf_b, sem_b)
    finish(0, gbuf_a, sem_a)

    @pl.when(i + 1 < pl.num_programs(0))
    def _():
        start_gathers(2 * i + 2, gbuf_a, sem_a)

    finish(1, gbuf_b, sem_b)


def combine(pos_flat, y_slab, x2d, routed, g, b):
    t, d = x2d.shape
    tm = 2 * TM_COMB
    gbuf = pltpu.VMEM((2 * TM_COMB * GATHER_PITCH, LANES), F32)
    return pl.pallas_call(
        _combine_body,
        grid_spec=pltpu.PrefetchScalarGridSpec(
            num_scalar_prefetch=1,
            grid=(t // tm,),
            in_specs=[pl.BlockSpec(memory_space=pl.ANY),
                      pl.BlockSpec((tm, d), lambda i, pos: (i, 0)),
                      pl.BlockSpec((tm, LANES), lambda i, pos: (i, 0)),
                      pl.BlockSpec((1, d), lambda i, pos: (0, 0)),
                      pl.BlockSpec((1, d), lambda i, pos: (0, 0))],
            out_specs=pl.BlockSpec((tm, d), lambda i, pos: (i, 0)),
            scratch_shapes=[gbuf, gbuf, pltpu.VMEM((TM_COMB, d), F32),
                            pltpu.SemaphoreType.DMA, pltpu.SemaphoreType.DMA],
        ),
        out_shape=jax.ShapeDtypeStruct((t, d), F32),
        compiler_params=_cparams(("arbitrary",)),
        name="moe_combine",
    )(pos_flat, y_slab, x2d, routed, g, b)


def moe_ffn(x2d, x_slab, w_router_pad, wg, wu, wd, j, g, b):
    t, d = x2d.shape
    tm = TM_FFN
    n_tiles = -(-(t * 2) // tm) + N_EXPERTS
    routed, counts = router(x2d, w_router_pad)
    idx = routed[:, _L_IDX:_L_IDX + 2].astype(jnp.int32)
    rank = routed[:, _L_RANK:_L_RANK + 2].astype(jnp.int32)
    counts = counts[0, :N_EXPERTS].astype(jnp.int32)
    tiles_per_e = (counts + tm - 1) // tm
    tile_end = jnp.cumsum(tiles_per_e)
    tile_start = tile_end - tiles_per_e
    pos = ((tile_start * tm)[idx] + rank).reshape(-1)
    n_valid = tile_end[-1:]
    tile_id = jnp.minimum(jnp.arange(n_tiles, dtype=jnp.int32), n_valid[0] - 1)
    tile_expert = jnp.sum(tile_id[:, None] >= tile_end[None, :], axis=1).astype(jnp.int32)
    tile_rows = jnp.minimum(counts[tile_expert] - (tile_id - tile_start[tile_expert]) * tm, tm).astype(jnp.int32)
    row_token = invert_positions(pos, n_tiles * tm)
    y_slab = grouped_experts(tile_expert, n_valid.astype(jnp.int32), tile_rows, row_token, x_slab, wg, wu, wd, j)
    return combine(pos, y_slab, x2d, routed, g, b)


def mixer_weights(w_in, w_q_b, w_kv_b):
    depth = w_in.shape[0]
    win = jnp.pad(w_in, ((0, 0), (0, 0), (0, LANES - QK_ROPE_DIM))).astype(BF16)
    wq = w_q_b.reshape(depth, Q_LORA_RANK, N_HEADS, QK_NOPE_DIM + QK_ROPE_DIM)
    wq = jnp.pad(wq, ((0, 0), (0, 0), (0, 0), (0, HEAD_PAD - QK_NOPE_DIM - QK_ROPE_DIM)))
    wq = wq.reshape(depth, Q_LORA_RANK, N_HEADS * HEAD_PAD).astype(BF16)
    wkv = w_kv_b.reshape(depth, KV_LORA_RANK, N_HEADS, QK_NOPE_DIM + V_HEAD_DIM)
    wk = wkv[..., :QK_NOPE_DIM].reshape(depth, KV_LORA_RANK, N_HEADS * QK_NOPE_DIM).astype(BF16)
    wv = wkv[..., QK_NOPE_DIM:].reshape(depth, KV_LORA_RANK, N_HEADS * V_HEAD_DIM).astype(BF16)
    return win, wq, wk, wv


def kernel(x, positions, w_in, conv_dw, conv_gn_g, conv_gn_b, q_norm_g, w_q_b, kv_norm_g, w_kv_b, w_o, ln_mix_g, ln_mix_b, ffn_w_gate, ffn_w_up, ffn_w_down, router_w, exp_w_gate, exp_w_up, exp_w_down, ln_ffn_g, ln_ffn_b):
    bsz, seq, d = x.shape
    assert d == SLAB * LANES
    t = bsz * seq
    depth = w_in.shape[0]
    d_conv = conv_dw.shape[2]

    win, wq, wk, wv = mixer_weights(w_in, w_q_b, w_kv_b)
    wo = w_o.astype(BF16)
    wr = jnp.pad(router_w, ((0, 0), (0, 0), (0, LANES - N_EXPERTS)))

    rope_c, rope_sn, rope_sp = rope_tables(positions.reshape(t, 1))

    x2d = x.reshape(t, d)
    for l in range(depth):
        u_conv, qt, k, vt = in_proj(x2d, seq, win[l], q_norm_g[l][None], wq[l], kv_norm_g[l][None], wk[l], wv[l],
                                    rope_c, rope_sn, rope_sp)
        y_conv = conv_module(u_conv.reshape(bsz, seq, 2 * d_conv), conv_dw[l], conv_gn_g[l][None], conv_gn_b[l][None])
        y_att = attention(qt, k.reshape(bsz, seq, -1), vt)
        is_moe = l % 2 == 1
        mixed = out_proj(y_conv.reshape(t, d_conv), y_att.reshape(t, -1), wo[l, :d_conv], wo[l, d_conv:], x2d,
                         ln_mix_g[l][None], ln_mix_b[l][None], with_slab=is_moe)
        j = l // 2
        if is_moe:
            x2d = moe_ffn(mixed[0], mixed[1], wr[j], exp_w_gate, exp_w_up, exp_w_down, j,
                          ln_ffn_g[l][None], ln_ffn_b[l][None])
        else:
            x2d = dense_ffn(mixed[0], ffn_w_gate, ffn_w_up, ffn_w_down, j, ln_ffn_g[l][None], ln_ffn_b[l][None])
    return x2d.reshape(bsz, seq, d)
```

```python
import functools
import math

import jax
import jax.numpy as jnp
from jax import lax
from jax.experimental import pallas as pl
from jax.experimental.pallas import tpu as pltpu

F32 = jnp.float32
BF16 = jnp.bfloat16

CONV_GROUPS = 8
CONV_WIDTH = 31
N_HEADS = 8
QK_NOPE_DIM = 128
QK_ROPE_DIM = 64
V_HEAD_DIM = 128
Q_LORA_RANK = 512
KV_LORA_RANK = 256
ROPE_THETA = 10000.0
N_EXPERTS = 8
DEPTH = 4
ALPHA = (2.0 * DEPTH) ** 0.25
LN_EPS = 1e-5
RMS_EPS = 1e-6

LANES = 128
SUBLANES = 8
VMEM_LIMIT = 56 * 1024 * 1024

HEAD_PAD = 2 * LANES
VT_ROWS = 128 + 16
CONV_HALO = 32

TM_PROJ = 512
TS_CONV = 256
TQ_ATT = 512
HEADS_PER_STEP = 4
TM_OUT = 512
TM_FFN = 1024
TM_HALF = TM_FFN // 2
TF_FFN = 256
TF_DENSE = 512
TM_ROUTE = 512
TM_COMB = 256
INV_UNROLL = 16
SLAB = 2048 // LANES
GATHER_PITCH = SLAB + SUBLANES


def _cparams(sem):
    return pltpu.CompilerParams(dimension_semantics=sem, vmem_limit_bytes=VMEM_LIMIT)


def _const_spec(shape):
    nd = len(shape)
    return pl.BlockSpec(shape, lambda *_: (0,) * nd, pipeline_mode=pl.Buffered(1))


def _layer_norm(z, g, b):
    mu = jnp.mean(z, axis=-1, keepdims=True)
    d = z - mu
    var = jnp.mean(d * d, axis=-1, keepdims=True)
    return d * lax.rsqrt(var + LN_EPS) * g + b


def _rope(x, c, sn, sp):
    return x * c + pltpu.roll(x, LANES - QK_ROPE_DIM // 2, 1) * sn + pltpu.roll(x, QK_ROPE_DIM // 2, 1) * sp


def _rope_table_body(pos_ref, invf_ref, c_ref, sn_ref, sp_ref):
    ang = pos_ref[...].astype(F32) * invf_ref[...]
    c = jnp.cos(ang)
    s = jnp.sin(ang)
    lane = lax.broadcasted_iota(jnp.int32, ang.shape, 1)
    half = QK_ROPE_DIM // 2
    c_ref[...] = jnp.where(lane < QK_ROPE_DIM, c, 0.0)
    sn_ref[...] = jnp.where(lane < half, -s, 0.0)
    sp_ref[...] = jnp.where((lane >= half) & (lane < QK_ROPE_DIM), s, 0.0)


def rope_tables(pos_col):
    t = pos_col.shape[0]
    tm = 1024
    half = QK_ROPE_DIM // 2
    inv_freq = ROPE_THETA ** (-jnp.arange(0, QK_ROPE_DIM, 2, dtype=F32) / QK_ROPE_DIM)
    invf = jnp.concatenate([inv_freq, inv_freq, jnp.zeros((LANES - 2 * half,), F32)]).reshape(1, LANES)
    out = jax.ShapeDtypeStruct((t, LANES), F32)
    row = pl.BlockSpec((tm, LANES), lambda i: (i, 0))
    return pl.pallas_call(
        _rope_table_body,
        grid=(t // tm,),
        in_specs=[pl.BlockSpec((tm, 1), lambda i: (i, 0)), pl.BlockSpec((1, LANES), lambda i: (0, 0))],
        out_specs=[row, row, row],
        out_shape=[out, out, out],
        compiler_params=_cparams(("arbitrary",)),
        name="rope_tables",
    )(pos_col, invf)


def _in_proj_body(x_ref, win_ref, qg_ref, wq_ref, kvg_ref, wk_ref, wv_ref, c_ref, sn_ref, sp_ref,
                  uconv_ref, qt_ref, k_ref, vt_ref, *, d_conv2, scale):
    xb = x_ref[...].astype(BF16)
    c, sn, sp = c_ref[...], sn_ref[...], sp_ref[...]
    o_q = d_conv2
    o_kv = o_q + Q_LORA_RANK
    o_kr = o_kv + KV_LORA_RANK

    uconv_ref[...] = jnp.dot(xb, win_ref[:, :d_conv2], preferred_element_type=F32)

    uq = jnp.dot(xb, win_ref[:, o_q:o_kv], preferred_element_type=F32)
    ms = jnp.mean(uq * uq, axis=-1, keepdims=True)
    q_lat = (uq * lax.rsqrt(ms + RMS_EPS) * qg_ref[...]).astype(BF16)
    q = jnp.dot(q_lat, wq_ref[...], preferred_element_type=F32) * scale
    for h in range(N_HEADS):
        lo = h * HEAD_PAD
        qt_ref[0, lo:lo + LANES, :] = q[:, lo:lo + LANES].T.astype(BF16)
        qt_ref[0, lo + LANES:lo + HEAD_PAD, :] = _rope(q[:, lo + LANES:lo + HEAD_PAD], c, sn, sp).T.astype(BF16)

    ukv = jnp.dot(xb, win_ref[:, o_kv:o_kr], preferred_element_type=F32)
    ms = jnp.mean(ukv * ukv, axis=-1, keepdims=True)
    kv_lat = (ukv * lax.rsqrt(ms + RMS_EPS) * kvg_ref[...]).astype(BF16)
    v = jnp.dot(kv_lat, wv_ref[...], preferred_element_type=F32)
    ones = jnp.ones((VT_ROWS - V_HEAD_DIM, v.shape[0]), BF16)
    for h in range(N_HEADS):
        lo = h * VT_ROWS
        vt_ref[0, 0, lo:lo + V_HEAD_DIM, :] = v[:, h * V_HEAD_DIM:(h + 1) * V_HEAD_DIM].T.astype(BF16)
        vt_ref[0, 0, lo + V_HEAD_DIM:lo + VT_ROWS, :] = ones
    kn = jnp.dot(kv_lat, wk_ref[...], preferred_element_type=F32)
    ukr = jnp.dot(xb, win_ref[:, o_kr:o_kr + LANES], preferred_element_type=F32)
    kr = _rope(ukr, c, sn, sp).astype(BF16)
    for h in range(N_HEADS):
        lo = h * HEAD_PAD
        k_ref[:, lo:lo + LANES] = kn[:, h * LANES:(h + 1) * LANES].astype(BF16)
        k_ref[:, lo + LANES:lo + HEAD_PAD] = kr


def in_proj(x2d, seq, win, qg, wq, kvg, wk, wv, rope_c, rope_sn, rope_sp):
    t, d = x2d.shape
    tm = TM_PROJ
    assert tm == TQ_ATT and seq % tm == 0
    nb = seq // tm
    bsz = t // seq
    d_conv2 = win.shape[1] - (Q_LORA_RANK + KV_LORA_RANK + LANES)
    scale = float((QK_NOPE_DIM + QK_ROPE_DIM) ** -0.5 * math.log2(math.e))
    row = lambda w: pl.BlockSpec((tm, w), lambda i: (i, 0))
    return pl.pallas_call(
        functools.partial(_in_proj_body, d_conv2=d_conv2, scale=scale),
        grid=(t // tm,),
        in_specs=[row(d), _const_spec(win.shape), _const_spec(qg.shape), _const_spec(wq.shape),
                  _const_spec(kvg.shape), _const_spec(wk.shape), _const_spec(wv.shape),
                  row(LANES), row(LANES), row(LANES)],
        out_specs=[row(d_conv2),
                   pl.BlockSpec((1, N_HEADS * HEAD_PAD, tm), lambda i: (i // nb, 0, i % nb)),
                   row(N_HEADS * HEAD_PAD),
                   pl.BlockSpec((1, 1, N_HEADS * VT_ROWS, tm), lambda i: (i // nb, i % nb, 0, 0))],
        out_shape=[jax.ShapeDtypeStruct((t, d_conv2), F32),
                   jax.ShapeDtypeStruct((bsz, N_HEADS * HEAD_PAD, seq), BF16),
                   jax.ShapeDtypeStruct((t, N_HEADS * HEAD_PAD), BF16),
                   jax.ShapeDtypeStruct((bsz, nb, N_HEADS * VT_ROWS, tm), BF16)],
        compiler_params=_cparams(("arbitrary",)),
        name="in_proj",
    )(x2d, win, qg, wq, kvg, wk, wv, rope_c, rope_sn, rope_sp)


def _conv_body(a_ref, g_ref, w_ref, gng_ref, gnb_ref, o_ref, hext, shifted):
    ts = a_ref.shape[1]
    s = pl.program_id(1)

    @pl.when(s == 0)
    def _():
        hext[0:CONV_HALO, :] = jnp.zeros((CONV_HALO, hext.shape[1]), F32)

    @pl.when(s > 0)
    def _():
        hext[0:CONV_HALO, :] = hext[ts:ts + CONV_HALO, :]

    hext[CONV_HALO:CONV_HALO + ts, :] = a_ref[0] * jax.nn.sigmoid(g_ref[0])

    first = CONV_HALO - (CONV_WIDTH - 1)
    n_groups = hext.shape[1] // LANES
    for c in range(n_groups):
        lanes = slice(c * LANES, (c + 1) * LANES)
        acc = None
        for b in range(SUBLANES):
            offs = [o for o in range(first, first + CONV_WIDTH) if o % SUBLANES == b]
            if b:
                n = offs[-1] - b + ts
                shifted[b, 0:n, :] = hext[b:b + n, lanes]
            for o in offs:
                hb = shifted[b, o - b:o - b + ts, :] if b else hext[o:o + ts, lanes]
                term = hb * w_ref[o - first:o - first + 1, lanes]
                acc = term if acc is None else acc + term
        mu = jnp.mean(acc, axis=-1, keepdims=True)
        d = acc - mu
        var = jnp.mean(d * d, axis=-1, keepdims=True)
        y = d * lax.rsqrt(var + LN_EPS) * gng_ref[:, lanes] + gnb_ref[:, lanes]
        o_ref[0, :, lanes] = (y * jax.nn.sigmoid(y)).astype(BF16)


def conv_module(u_conv, w_dw, gn_g, gn_b):
    b, s, d2 = u_conv.shape
    dc = d2 // 2
    assert dc // CONV_GROUPS == LANES
    ts = TS_CONV
    return pl.pallas_call(
        _conv_body,
        grid=(b, s // ts),
        in_specs=[pl.BlockSpec((1, ts, dc), lambda i, j: (i, j, 0)),
                  pl.BlockSpec((1, ts, dc), lambda i, j: (i, j, 1)),
                  pl.BlockSpec((CONV_WIDTH, dc), lambda i, j: (0, 0)),
                  pl.BlockSpec((1, dc), lambda i, j: (0, 0)),
                  pl.BlockSpec((1, dc), lambda i, j: (0, 0))],
        out_specs=pl.BlockSpec((1, ts, dc), lambda i, j: (i, j, 0)),
        out_shape=jax.ShapeDtypeStruct((b, s, dc), BF16),
        scratch_shapes=[pltpu.VMEM((CONV_HALO + ts, dc), F32), pltpu.VMEM((SUBLANES, CONV_HALO + ts, LANES), F32)],
        compiler_params=_cparams(("arbitrary", "arbitrary")),
        name="conv_module",
    )(u_conv, u_conv, w_dw, gn_g, gn_b)


def _attn_body(qt_ref, k_ref, vt_ref, o_ref, m_sc, acc_sc):
    tq = qt_ref.shape[2]
    i = pl.program_id(2)
    m_sc[...] = jnp.full(m_sc.shape, -jnp.inf, F32)
    acc_sc[...] = jnp.zeros(acc_sc.shape, F32)

    def block(j, masked):
        start = pl.multiple_of(j * tq, tq)
        for h in range(HEADS_PER_STEP):
            dims = slice(h * HEAD_PAD, (h + 1) * HEAD_PAD)
            st = jnp.dot(k_ref[0, pl.ds(start, tq), dims], qt_ref[0, dims, :], preferred_element_type=F32)
            if masked:
                key = lax.broadcasted_iota(jnp.int32, st.shape, 0)
                query = lax.broadcasted_iota(jnp.int32, st.shape, 1)
                st = jnp.where(key <= query, st, -jnp.inf)
            m_prev = m_sc[h]
            m_new = jnp.maximum(m_prev, jnp.max(st, axis=0, keepdims=True))
            a = jnp.exp2(m_prev - m_new)
            pt = jnp.exp2(st - m_new).astype(BF16)
            vt = vt_ref[0, j, h * VT_ROWS:(h + 1) * VT_ROWS, :]
            acc_sc[h] = a * acc_sc[h] + jnp.dot(vt, pt, preferred_element_type=F32)
            m_sc[h] = m_new

    def body(j, carry):
        block(j, False)
        return carry

    lax.fori_loop(0, i, body, 0)
    block(i, True)
    for h in range(HEADS_PER_STEP):
        acc = acc_sc[h]
        out_t = acc[:V_HEAD_DIM, :] / acc[V_HEAD_DIM:V_HEAD_DIM + 1, :]
        o_ref[0, :, h * V_HEAD_DIM:(h + 1) * V_HEAD_DIM] = out_t.T.astype(BF16)


def attention(qt, k, vt):
    b, s, _ = k.shape
    tq = TQ_ATT
    hp = HEADS_PER_STEP
    return pl.pallas_call(
        _attn_body,
        grid=(b, N_HEADS // hp, s // tq),
        in_specs=[pl.BlockSpec((1, hp * HEAD_PAD, tq), lambda bi, h, i: (bi, h, i)),
                  pl.BlockSpec((1, s, hp * HEAD_PAD), lambda bi, h, i: (bi, 0, h)),
                  pl.BlockSpec((1, s // tq, hp * VT_ROWS, tq), lambda bi, h, i: (bi, 0, h, 0))],
        out_specs=pl.BlockSpec((1, tq, hp * V_HEAD_DIM), lambda bi, h, i: (bi, i, h)),
        out_shape=jax.ShapeDtypeStruct((b, s, N_HEADS * V_HEAD_DIM), BF16),
        scratch_shapes=[pltpu.VMEM((hp, 1, tq), F32), pltpu.VMEM((hp, VT_ROWS, tq), F32)],
        compiler_params=_cparams(("arbitrary", "arbitrary", "arbitrary")),
        name="attention",
    )(qt, k, vt)


def _out_proj_body(yc_ref, ya_ref, wc_ref, wa_ref, x_ref, g_ref, b_ref, o_ref, *maybe_slab_ref):
    y = jnp.dot(yc_ref[...], wc_ref[...], preferred_element_type=F32)
    y = y + jnp.dot(ya_ref[...], wa_ref[...], preferred_element_type=F32)
    o_ref[...] = _layer_norm(ALPHA * x_ref[...] + y, g_ref[...], b_ref[...])
    for slab_ref in maybe_slab_ref:
        tm = o_ref.shape[0]
        for c in range(SLAB):
            slab_ref[pl.ds(c, tm, stride=SLAB), :] = o_ref[:, c * LANES:(c + 1) * LANES]


def out_proj(y_conv, y_att, wo_conv, wo_att, x2d, g, b, with_slab):
    t, d = x2d.shape
    tm = TM_OUT
    row = lambda w: pl.BlockSpec((tm, w), lambda i: (i, 0))
    out_specs = [row(d)]
    out_shape = [jax.ShapeDtypeStruct((t, d), F32)]
    if with_slab:
        out_specs.append(pl.BlockSpec((tm * SLAB, LANES), lambda i: (i, 0)))
        out_shape.append(jax.ShapeDtypeStruct((t * SLAB, LANES), F32))
    return pl.pallas_call(
        _out_proj_body,
        grid=(t // tm,),
        in_specs=[row(y_conv.shape[1]), row(y_att.shape[1]), _const_spec(wo_conv.shape), _const_spec(wo_att.shape),
                  row(d), _const_spec(g.shape), _const_spec(b.shape)],
        out_specs=out_specs,
        out_shape=out_shape,
        compiler_params=_cparams(("arbitrary",)),
        name="out_proj",
    )(y_conv, y_att, wo_conv, wo_att, x2d, g, b)


def _swiglu_chunk(xb, wg, wu, wd):
    g = jnp.dot(xb, wg, preferred_element_type=F32)
    u = jnp.dot(xb, wu, preferred_element_type=F32)
    h = (g * jax.nn.sigmoid(g) * u).astype(BF16)
    return jnp.dot(h, wd, preferred_element_type=F32)


def _half_rows(hf):
    return slice(hf * TM_HALF, (hf + 1) * TM_HALF)


def _ffn_body(x_ref, wg_ref, wu_ref, wd_ref, g_ref, b_ref, o_ref, xb_sc):
    f = pl.program_id(1)

    @pl.when(f == 0)
    def _():
        xb_sc[...] = x_ref[...].astype(BF16)
        o_ref[...] = jnp.zeros(o_ref.shape, F32)

    wg, wu, wd = wg_ref[...].astype(BF16), wu_ref[...].astype(BF16), wd_ref[...].astype(BF16)
    for hf in range(TM_FFN // TM_HALF):
        rows = _half_rows(hf)
        o_ref[rows, :] += _swiglu_chunk(xb_sc[rows, :], wg, wu, wd)

    @pl.when(f == pl.num_programs(1) - 1)
    def _():
        for hf in range(TM_FFN // TM_HALF):
            rows = _half_rows(hf)
            o_ref[rows, :] = _layer_norm(ALPHA * x_ref[rows, :] + o_ref[rows, :], g_ref[...], b_ref[...])


def dense_ffn(x2d, wg, wu, wd, j, g, b):
    t, d = x2d.shape
    dff = wg.shape[2]
    tm, tf = TM_FFN, TF_DENSE
    return pl.pallas_call(
        _ffn_body,
        grid=(t // tm, dff // tf),
        in_specs=[pl.BlockSpec((tm, d), lambda i, f: (i, 0), pipeline_mode=pl.Buffered(1)),
                  pl.BlockSpec((None, d, tf), lambda i, f: (j, 0, f)),
                  pl.BlockSpec((None, d, tf), lambda i, f: (j, 0, f)),
                  pl.BlockSpec((None, tf, d), lambda i, f: (j, f, 0)),
                  pl.BlockSpec((1, d), lambda i, f: (0, 0)),
                  pl.BlockSpec((1, d), lambda i, f: (0, 0))],
        out_specs=pl.BlockSpec((tm, d), lambda i, f: (i, 0), pipeline_mode=pl.Buffered(1)),
        out_shape=jax.ShapeDtypeStruct((t, d), F32),
        scratch_shapes=[pltpu.VMEM((tm, d), BF16)],
        compiler_params=_cparams(("arbitrary", "arbitrary")),
        name="dense_ffn",
    )(x2d, wg, wu, wd, g, b)


_L_IDX, _L_RANK, _L_GATE = 0, 2, 4


def _router_body(x_ref, wr_ref, o_ref, cnt_ref, base_sc):
    tm = x_ref.shape[0]

    @pl.when(pl.program_id(0) == 0)
    def _():
        base_sc[...] = jnp.zeros(base_sc.shape, F32)

    x = x_ref[...]
    w = wr_ref[...]
    xh = x.astype(BF16)
    xl = (x - xh.astype(F32)).astype(BF16)
    wh = w.astype(BF16)
    wl = (w - wh.astype(F32)).astype(BF16)
    logits = (jnp.dot(xh, wh, preferred_element_type=F32) + jnp.dot(xl, wh, preferred_element_type=F32)
              + jnp.dot(xh, wl, preferred_element_type=F32))
    lane = lax.broadcasted_iota(jnp.int32, logits.shape, 1).astype(F32)
    lg = jnp.where(lane < N_EXPERTS, logits, -jnp.inf)
    m1 = jnp.max(lg, axis=-1, keepdims=True)
    i1 = jnp.min(jnp.where(lg == m1, lane, float(LANES)), axis=-1, keepdims=True)
    oh1 = lane == i1
    lg2 = jnp.where(oh1, -jnp.inf, lg)
    m2 = jnp.max(lg2, axis=-1, keepdims=True)
    i2 = jnp.min(jnp.where(lg2 == m2, lane, float(LANES)), axis=-1, keepdims=True)
    oh2 = lane == i2
    e2 = jnp.exp(m2 - m1)
    g1 = 1.0 / (1.0 + e2)
    g2 = e2 / (1.0 + e2)
    cnt = jnp.where(oh1 | oh2, 1.0, 0.0)
    r = lax.broadcasted_iota(jnp.int32, (tm, tm), 0)
    c = lax.broadcasted_iota(jnp.int32, (tm, tm), 1)
    strict_lower = jnp.where(c < r, 1.0, 0.0).astype(BF16)
    before = jnp.dot(strict_lower, cnt.astype(BF16), preferred_element_type=F32) + base_sc[...]
    r1 = jnp.sum(jnp.where(oh1, before, 0.0), axis=-1, keepdims=True)
    r2 = jnp.sum(jnp.where(oh2, before, 0.0), axis=-1, keepdims=True)
    base_sc[...] += jnp.sum(cnt, axis=0, keepdims=True)
    cnt_ref[...] = base_sc[...]
    packed = jnp.zeros(logits.shape, F32)
    for ln, val in ((_L_IDX, i1), (_L_IDX + 1, i2), (_L_RANK, r1), (_L_RANK + 1, r2), (_L_GATE, g1), (_L_GATE + 1, g2)):
        packed = jnp.where(lane == float(ln), val, packed)
    o_ref[...] = packed


def router(x2d, w_router_pad):
    t, d = x2d.shape
    tm = TM_ROUTE
    return pl.pallas_call(
        _router_body,
        grid=(t // tm,),
        in_specs=[pl.BlockSpec((tm, d), lambda i: (i, 0)), pl.BlockSpec((d, LANES), lambda i: (0, 0))],
        out_specs=[pl.BlockSpec((tm, LANES), lambda i: (i, 0)), pl.BlockSpec((1, LANES), lambda i: (0, 0))],
        out_shape=[jax.ShapeDtypeStruct((t, LANES), F32), jax.ShapeDtypeStruct((1, LANES), F32)],
        scratch_shapes=[pltpu.VMEM((1, LANES), F32)],
        compiler_params=_cparams(("arbitrary",)),
        name="router",
    )(x2d, w_router_pad)


def _invert_body(pos_ref, o_ref):
    def zero(r8, carry):
        for u in range(INV_UNROLL):
            o_ref[r8 * INV_UNROLL + u] = 0
        return carry

    lax.fori_loop(0, o_ref.shape[0] // INV_UNROLL, zero, 0)

    def put(a8, carry):
        for u in range(INV_UNROLL):
            a = a8 * INV_UNROLL + u
            o_ref[pos_ref[a]] = lax.shift_right_logical(a, 1)
        return carry

    lax.fori_loop(0, pos_ref.shape[0] // INV_UNROLL, put, 0)


def invert_positions(pos_flat, n_rows):
    return pl.pallas_call(
        _invert_body,
        in_specs=[pl.BlockSpec(memory_space=pltpu.SMEM)],
        out_specs=pl.BlockSpec(memory_space=pltpu.SMEM),
        out_shape=jax.ShapeDtypeStruct((n_rows,), jnp.int32),
        name="invert_positions",
    )(pos_flat)


def _slab_rows(row):
    start = row * SLAB
    return pl.ds(start if isinstance(start, int) else pl.multiple_of(start, SLAB), SLAB)


def _gather_copy(src_hbm, src_row, gbuf, slot, sem):
    start = slot * GATHER_PITCH
    start = start if isinstance(start, int) else pl.multiple_of(start, SUBLANES)
    return pltpu.make_async_copy(src_hbm.at[_slab_rows(src_row), :], gbuf.at[pl.ds(start, SLAB), :], sem)


def _wait_gathers(src_hbm, gbuf, sem):
    rows = gbuf.shape[0] // GATHER_PITCH * SLAB
    pltpu.make_async_copy(src_hbm.at[pl.ds(0, rows), :], gbuf.at[pl.ds(0, rows), :], sem).wait()


def _gathered_chunk(gbuf, first_slot, n, c):
    return gbuf[pl.ds(first_slot * GATHER_PITCH + c, n, stride=GATHER_PITCH), :]


def _experts_body(te_ref, nv_ref, tr_ref, rt_ref, x_hbm, wg_ref, wu_ref, wd_ref, o_ref,
                  gbuf, xb_sc, acc_sc, sem, *, per_step):
    tm = TM_FFN
    i = pl.program_id(0)
    f = pl.program_id(1)
    nf = pl.num_programs(1)
    valid = i < nv_ref[0]
    last_row = rt_ref.shape[0] - 1

    def start_gather(tile, slot):
        token = rt_ref[jnp.minimum(tile * tm + slot, last_row)]
        _gather_copy(x_hbm, token, gbuf, slot, sem).start()

    def wait_gathers():
        _wait_gathers(x_hbm, gbuf, sem)

    @pl.when(valid & (f == 0))
    def _():
        @pl.when(i == 0)
        def _():
            def first_tile(slot, carry):
                start_gather(0, slot)
                return carry

            lax.fori_loop(0, gbuf.shape[0] // GATHER_PITCH, first_tile, 0)

        wait_gathers()
        for c in range(SLAB):
            xb_sc[:, c * LANES:(c + 1) * LANES] = _gathered_chunk(gbuf, 0, tm, c).astype(BF16)
        acc_sc[...] = jnp.zeros(acc_sc.shape, F32)

    @pl.when(jnp.logical_not(valid) & (f == 0))
    def _():
        o_ref[...] = jnp.zeros(o_ref.shape, F32)

    @pl.when(valid)
    def _():
        nxt = jnp.minimum(i + 1, pl.num_programs(0) - 1)
        for u in range(per_step):
            start_gather(nxt, f * per_step + u)
        wg, wu, wd = wg_ref[...].astype(BF16), wu_ref[...].astype(BF16), wd_ref[...].astype(BF16)
        rows = _half_rows(0)
        acc_sc[rows, :] += _swiglu_chunk(xb_sc[rows, :], wg, wu, wd)

        @pl.when(tr_ref[i] > TM_HALF)
        def _():
            rows = _half_rows(1)
            acc_sc[rows, :] += _swiglu_chunk(xb_sc[rows, :], wg, wu, wd)

    @pl.when(valid & (f == nf - 1))
    def _():
        for c in range(SLAB):
            o_ref[pl.ds(c, tm, stride=SLAB), :] = acc_sc[:, c * LANES:(c + 1) * LANES]

        @pl.when(i == nv_ref[0] - 1)
        def _():
            wait_gathers()


def grouped_experts(tile_expert, n_valid, tile_rows, row_token, x_slab, wg, wu, wd, j):
    n_tiles = tile_expert.shape[0]
    d, dff = wg.shape[2], wg.shape[3]
    tm, tf = TM_FFN, TF_FFN
    nf = dff // tf
    per_step = -(-tm // nf)

    def f_idx(i, f, nv):
        return jnp.where(i < nv[0], f, nf - 1)

    return pl.pallas_call(
        functools.partial(_experts_body, per_step=per_step),
        grid_spec=pltpu.PrefetchScalarGridSpec(
            num_scalar_prefetch=4,
            grid=(n_tiles, nf),
            in_specs=[pl.BlockSpec(memory_space=pl.ANY),
                      pl.BlockSpec((None, None, d, tf), lambda i, f, te, nv, tr, rt: (j, te[i], 0, f_idx(i, f, nv))),
                      pl.BlockSpec((None, None, d, tf), lambda i, f, te, nv, tr, rt: (j, te[i], 0, f_idx(i, f, nv))),
                      pl.BlockSpec((None, None, tf, d), lambda i, f, te, nv, tr, rt: (j, te[i], f_idx(i, f, nv), 0))],
            out_specs=pl.BlockSpec((tm * SLAB, LANES), lambda i, f, te, nv, tr, rt: (i, 0),
                                   pipeline_mode=pl.Buffered(1)),
            scratch_shapes=[pltpu.VMEM((nf * per_step * GATHER_PITCH, LANES), F32), pltpu.VMEM((tm, d), BF16),
                            pltpu.VMEM((tm, d), F32), pltpu.SemaphoreType.DMA],
        ),
        out_shape=jax.ShapeDtypeStruct((n_tiles * tm * SLAB, LANES), F32),
        compiler_params=_cparams(("arbitrary", "arbitrary")),
        name="grouped_experts",
    )(tile_expert, n_valid, tile_rows, row_token, x_slab, wg, wu, wd)


def _combine_body(pos_ref, y_hbm, x_ref, rt_ref, g_ref, b_ref, o_ref, gbuf_a, gbuf_b, z_sc, sem_a, sem_b):
    tm = TM_COMB
    i = pl.program_id(0)

    def start_gathers(sub_tile, gbuf, sem):
        def group(gi, carry):
            for u in range(INV_UNROLL):
                r = gi * INV_UNROLL + u
                t = sub_tile * tm + r
                _gather_copy(y_hbm, pos_ref[2 * t], gbuf, r, sem).start()
                _gather_copy(y_hbm, pos_ref[2 * t + 1], gbuf, tm + r, sem).start()
            return carry

        lax.fori_loop(0, tm // INV_UNROLL, group, 0)

    def finish(half, gbuf, sem):
        _wait_gathers(y_hbm, gbuf, sem)
        rows = slice(half * tm, (half + 1) * tm)
        g1 = rt_ref[rows, _L_GATE:_L_GATE + 1]
        g2 = rt_ref[rows, _L_GATE + 1:_L_GATE + 2]
        for c in range(SLAB):
            lanes = slice(c * LANES, (c + 1) * LANES)
            y1 = _gathered_chunk(gbuf, 0, tm, c)
            y2 = _gathered_chunk(gbuf, tm, tm, c)
            z_sc[:, lanes] = ALPHA * x_ref[rows, lanes] + (g1 * y1 + g2 * y2)
        o_ref[rows, :] = _layer_norm(z_sc[...], g_ref[...], b_ref[...])

    @pl.when(i == 0)
    def _():
        start_gathers(0, gbuf_a, sem_a)

    start_gathers(2 * i + 1, gbuf_b, sem_b)
    finish(0, gbuf_a, sem_a)

    @pl.when(i + 1 < pl.num_programs(0))
    def _():
        start_gathers(2 * i + 2, gbuf_a, sem_a)

    finish(1, gbuf_b, sem_b)


def combine(pos_flat, y_slab, x2d, routed, g, b):
    t, d = x2d.shape
    tm = 2 * TM_COMB
    gbuf = pltpu.VMEM((2 * TM_COMB * GATHER_PITCH, LANES), F32)
    return pl.pallas_call(
        _combine_body,
        grid_spec=pltpu.PrefetchScalarGridSpec(
            num_scalar_prefetch=1,
            grid=(t // tm,),
            in_specs=[pl.BlockSpec(memory_space=pl.ANY),
                      pl.BlockSpec((tm, d), lambda i, pos: (i, 0)),
                      pl.BlockSpec((tm, LANES), lambda i, pos: (i, 0)),
                      pl.BlockSpec((1, d), lambda i, pos: (0, 0)),
                      pl.BlockSpec((1, d), lambda i, pos: (0, 0))],
            out_specs=pl.BlockSpec((tm, d), lambda i, pos: (i, 0)),
            scratch_shapes=[gbuf, gbuf, pltpu.VMEM((TM_COMB, d), F32),
                            pltpu.SemaphoreType.DMA, pltpu.SemaphoreType.DMA],
        ),
        out_shape=jax.ShapeDtypeStruct((t, d), F32),
        compiler_params=_cparams(("arbitrary",)),
        name="moe_combine",
    )(pos_flat, y_slab, x2d, routed, g, b)


def moe_ffn(x2d, x_slab, w_router_pad, wg, wu, wd, j, g, b):
    t, d = x2d.shape
    tm = TM_FFN
    n_tiles = -(-(t * 2) // tm) + N_EXPERTS
    routed, counts = router(x2d, w_router_pad)
    idx = routed[:, _L_IDX:_L_IDX + 2].astype(jnp.int32)
    rank = routed[:, _L_RANK:_L_RANK + 2].astype(jnp.int32)
    counts = counts[0, :N_EXPERTS].astype(jnp.int32)
    tiles_per_e = (counts + tm - 1) // tm
    tile_end = jnp.cumsum(tiles_per_e)
    tile_start = tile_end - tiles_per_e
    pos = ((tile_start * tm)[idx] + rank).reshape(-1)
    n_valid = tile_end[-1:]
    tile_id = jnp.minimum(jnp.arange(n_tiles, dtype=jnp.int32), n_valid[0] - 1)
    tile_expert = jnp.sum(tile_id[:, None] >= tile_end[None, :], axis=1).astype(jnp.int32)
    tile_rows = jnp.minimum(counts[tile_expert] - (tile_id - tile_start[tile_expert]) * tm, tm).astype(jnp.int32)
    row_token = invert_positions(pos, n_tiles * tm)
    y_slab = grouped_experts(tile_expert, n_valid.astype(jnp.int32), tile_rows, row_token, x_slab, wg, wu, wd, j)
    return combine(pos, y_slab, x2d, routed, g, b)


def mixer_weights(w_in, w_q_b, w_kv_b):
    depth = w_in.shape[0]
    win = jnp.pad(w_in, ((0, 0), (0, 0), (0, LANES - QK_ROPE_DIM))).astype(BF16)
    wq = w_q_b.reshape(depth, Q_LORA_RANK, N_HEADS, QK_NOPE_DIM + QK_ROPE_DIM)
    wq = jnp.pad(wq, ((0, 0), (0, 0), (0, 0), (0, HEAD_PAD - QK_NOPE_DIM - QK_ROPE_DIM)))
    wq = wq.reshape(depth, Q_LORA_RANK, N_HEADS * HEAD_PAD).astype(BF16)
    wkv = w_kv_b.reshape(depth, KV_LORA_RANK, N_HEADS, QK_NOPE_DIM + V_HEAD_DIM)
    wk = wkv[..., :QK_NOPE_DIM].reshape(depth, KV_LORA_RANK, N_HEADS * QK_NOPE_DIM).astype(BF16)
    wv = wkv[..., QK_NOPE_DIM:].reshape(depth, KV_LORA_RANK, N_HEADS * V_HEAD_DIM).astype(BF16)
    return win, wq, wk, wv


def kernel(x, positions, w_in, conv_dw, conv_gn_g, conv_gn_b, q_norm_g, w_q_b, kv_norm_g, w_kv_b, w_o, ln_mix_g, ln_mix_b, ffn_w_gate, ffn_w_up, ffn_w_down, router_w, exp_w_gate, exp_w_up, exp_w_down, ln_ffn_g, ln_ffn_b):
    bsz, seq, d = x.shape
    assert d == SLAB * LANES
    t = bsz * seq
    depth = w_in.shape[0]
    d_conv = conv_dw.shape[2]

    win, wq, wk, wv = mixer_weights(w_in, w_q_b, w_kv_b)
    wo = w_o.astype(BF16)
    wr = jnp.pad(router_w, ((0, 0), (0, 0), (0, LANES - N_EXPERTS)))

    rope_c, rope_sn, rope_sp = rope_tables(positions.reshape(t, 1))

    x2d = x.reshape(t, d)
    for l in range(depth):
        u_conv, qt, k, vt = in_proj(x2d, seq, win[l], q_norm_g[l][None], wq[l], kv_norm_g[l][None], wk[l], wv[l],
                                    rope_c, rope_sn, rope_sp)
        y_conv = conv_module(u_conv.reshape(bsz, seq, 2 * d_conv), conv_dw[l], conv_gn_g[l][None], conv_gn_b[l][None])
        y_att = attention(qt, k.reshape(bsz, seq, -1), vt)
        is_moe = l % 2 == 1
        mixed = out_proj(y_conv.reshape(t, d_conv), y_att.reshape(t, -1), wo[l, :d_conv], wo[l, d_conv:], x2d,
                         ln_mix_g[l][None], ln_mix_b[l][None], with_slab=is_moe)
        j = l // 2
        if is_moe:
            x2d = moe_ffn(mixed[0], mixed[1], wr[j], exp_w_gate, exp_w_up, exp_w_down, j,
                          ln_ffn_g[l][None], ln_ffn_b[l][None])
        else:
            x2d = dense_ffn(mixed[0], ffn_w_gate, ffn_w_up, ffn_w_down, j, ln_ffn_g[l][None], ln_ffn_b[l][None])
    return x2d.reshape(bsz, seq, d)
```
